```python
import math
import jax, jax.numpy as jnp
from jax import lax
import numpy as np

D_MODEL = 1024
BATCH = 32
SEQ = 256
DEPTH = 2
DEC_BATCH = 2
DEC_SEQ = 2048
PAST_LEN = 512

GRID_W = 64
HEAD_DIM = 64
POOL_WIDTH = D_MODEL // 4
POOL_GROUPS = 4
POOL_GROUP_W = POOL_WIDTH // POOL_GROUPS
POOL_WINDOWS = (2, 4, 8, 16)
DIFF_HEADS = (D_MODEL // 2) // (2 * HEAD_DIM)
DIFF_QK_W = DIFF_HEADS * 2 * HEAD_DIM
DIFF_V_W = DIFF_HEADS * 2 * HEAD_DIM
NA_HEADS = (D_MODEL // 4) // HEAD_DIM
NA_W = NA_HEADS * HEAD_DIM
NA_ROWS = 8
NA_COLS = 16
N_BRANCH = 3
SPLIT_SIZES = (POOL_WIDTH, DIFF_QK_W, DIFF_QK_W, DIFF_V_W, NA_W, NA_W, NA_W, N_BRANCH * D_MODEL)
D_IN = POOL_WIDTH + 2 * DIFF_QK_W + DIFF_V_W + 3 * NA_W + N_BRANCH * D_MODEL
D_FF = ((8 * D_MODEL // 3 + 127) // 128) * 128
N_MOD = 9
QB = 128
ROPE_BASE = 10000.0
LN_EPS = 1e-5
RMS_EPS = 1e-5
ALPHA = (2 * DEPTH) ** 0.25
BETA = (8 * DEPTH) ** -0.25
ATTN_SCALE = HEAD_DIM ** -0.5
NEG_INF = -1e30

kernel_name = "hybrid_diffusion_pool_diffattn_natten_step"


def layer_norm(x, g, b):
    x32 = x.astype(jnp.float32)
    mu = jnp.mean(x32, axis=-1, keepdims=True)
    var = jnp.mean(jnp.square(x32 - mu), axis=-1, keepdims=True)
    return ((x32 - mu) * lax.rsqrt(var + LN_EPS) * g + b).astype(x.dtype)


def swiglu(h, w1, w3, w2):
    return (jax.nn.silu(h @ w1) * (h @ w3)) @ w2


def split_in(z):
    outs = []
    o = 0
    for s in SPLIT_SIZES:
        outs.append(z[..., o:o + s])
        o += s
    return outs


def rope2d(x):
    n = x.shape[1]
    t = jnp.arange(n)
    row = (t // GRID_W).astype(jnp.float32)
    col = (t % GRID_W).astype(jnp.float32)
    nf = HEAD_DIM // 4
    inv = ROPE_BASE ** (-jnp.arange(nf, dtype=jnp.float32) / nf)

    def rot(xh, pos):
        ang = (pos[:, None] * inv)[:, None, None, :]
        cos, sin = jnp.cos(ang), jnp.sin(ang)
        x1, x2 = xh[..., :nf], xh[..., nf:]
        return jnp.concatenate([x1 * cos - x2 * sin, x1 * sin + x2 * cos], axis=-1)

    x32 = x.astype(jnp.float32)
    out = jnp.concatenate([rot(x32[..., :HEAD_DIM // 2], row), rot(x32[..., HEAD_DIM // 2:], col)], axis=-1)
    return out.astype(x.dtype)


def multiscale_pool(a, w_pool, scale):
    n = a.shape[-2]
    af = a.astype(jnp.float32)
    cs = jnp.concatenate([jnp.zeros_like(af[..., :1, :]), jnp.cumsum(af, axis=-2)], axis=-2)
    t = jnp.arange(n)
    outs = []
    for g, w in enumerate(POOL_WINDOWS):
        lo = jnp.clip(t - w // 2, 0, n)
        hi = jnp.clip(t + w // 2, 0, n)
        seg = cs[..., g * POOL_GROUP_W:(g + 1) * POOL_GROUP_W]
        mean = (jnp.take(seg, hi, axis=-2) - jnp.take(seg, lo, axis=-2)) / (hi - lo).astype(jnp.float32)[:, None]
        outs.append(mean - af[..., g * POOL_GROUP_W:(g + 1) * POOL_GROUP_W])
    p = jnp.stack(outs, axis=-2)
    y = jnp.einsum('...ngc,gcd->...ngd', p, w_pool)
    y = y.reshape(*y.shape[:-2], POOL_WIDTH) * scale
    return y.astype(a.dtype)


def diff_attention(q, k, v, lam, lam_init, g):
    b, n = q.shape[:2]
    qb = q.reshape(b, n // QB, QB, *q.shape[2:]).swapaxes(0, 1)

    def one_block(qblk):
        s = jnp.einsum('bqhmd,bkhmd->bhmqk', qblk, k).astype(jnp.float32) * ATTN_SCALE
        p = jax.nn.softmax(s, axis=-1)
        a = (p[:, :, 0] - lam * p[:, :, 1]).astype(v.dtype)
        return jnp.einsum('bhqk,bkhe->bqhe', a, v)

    o = lax.map(one_block, qb).swapaxes(0, 1).reshape(b, n, *v.shape[2:])
    o32 = o.astype(jnp.float32)
    o32 = o32 * lax.rsqrt(jnp.mean(jnp.square(o32), axis=-1, keepdims=True) + RMS_EPS) * g * (1.0 - lam_init)
    return o32.astype(v.dtype).reshape(b, n, -1)


def dense_attention(q, k, v):
    b, n, h, hd = q.shape
    qb = q.reshape(b, n // QB, QB, h, hd).swapaxes(0, 1)

    def one_block(qblk):
        s = jnp.einsum('bqhd,bkhd->bhqk', qblk, k).astype(jnp.float32) * ATTN_SCALE
        p = jax.nn.softmax(s, axis=-1).astype(v.dtype)
        return jnp.einsum('bhqk,bkhd->bqhd', p, v)

    return lax.map(one_block, qb).swapaxes(0, 1).reshape(b, n, h * hd)


def neighbourhood_attention(q, k, v, ck, cv, rpb):
    b, l, h, hd = q.shape
    rows = l // GRID_W
    kh = min(NA_ROWS, rows)
    qg = q.reshape(b, rows, GRID_W, h, hd)
    kg = k.reshape(b, rows, GRID_W, h, hd)
    vg = v.reshape(b, rows, GRID_W, h, hd)
    r = jnp.arange(rows)
    row_start = jnp.clip(r - kh // 2, 0, rows - kh)
    row_idx = row_start[:, None] + jnp.arange(kh)
    k_nb = kg[:, row_idx]
    v_nb = vg[:, row_idx]
    col = jnp.arange(GRID_W)
    col_start = jnp.clip(col - NA_COLS // 2, 0, GRID_W - NA_COLS)
    col_ok = (col[None, :] >= col_start[:, None]) & (col[None, :] < col_start[:, None] + NA_COLS)
    dr = row_idx - r[:, None] + (NA_ROWS - 1)
    dc = jnp.clip(col[None, :] - col[:, None], -(NA_COLS - 1), NA_COLS - 1) + (NA_COLS - 1)
    bias = rpb[:, dr][:, :, :, dc]
    bias = bias.transpose(0, 1, 3, 2, 4).astype(jnp.float32)
    s_nb = jnp.einsum('brwhd,brkvhd->bhrwkv', qg, k_nb).astype(jnp.float32) * ATTN_SCALE + bias[None]
    s_nb = jnp.where(col_ok[:, None, :], s_nb, NEG_INF)
    s_ctx = jnp.einsum('brwhd,bchd->bhrwc', qg, ck).astype(jnp.float32) * ATTN_SCALE
    s = jnp.concatenate([s_nb.reshape(b, h, rows, GRID_W, kh * GRID_W), s_ctx], axis=-1)
    p = jax.nn.softmax(s, axis=-1).astype(v.dtype)
    p_nb = p[..., :kh * GRID_W].reshape(b, h, rows, GRID_W, kh, GRID_W)
    p_ctx = p[..., kh * GRID_W:]
    o = jnp.einsum('bhrwkv,brkvhd->brwhd', p_nb, v_nb) + jnp.einsum('bhrwc,bchd->brwhd', p_ctx, cv)
    return o.reshape(b, l, h * hd)


def setup_inputs(seed: int = 0) -> dict:
    key = jax.random.key(seed)
    ks = jax.random.split(key, 32)
    f32 = jnp.float32

    def nrm(k, shape, scale=1.0):
        return jax.random.normal(k, shape, f32) * scale

    return {
        'x_prompt': nrm(ks[0], (BATCH, SEQ, D_MODEL)),
        'x_sample': nrm(ks[1], (DEC_BATCH, DEC_SEQ, D_MODEL)),
        'cache_diff_k': nrm(ks[2], (DEC_BATCH, DEPTH, PAST_LEN, DIFF_HEADS, 2, HEAD_DIM)),
        'cache_diff_v': nrm(ks[3], (DEC_BATCH, DEPTH, PAST_LEN, DIFF_HEADS, 2 * HEAD_DIM)),
        'cache_na_k': nrm(ks[4], (DEC_BATCH, DEPTH, PAST_LEN, NA_HEADS, HEAD_DIM)),
        'cache_na_v': nrm(ks[5], (DEC_BATCH, DEPTH, PAST_LEN, NA_HEADS, HEAD_DIM)),
        'c': nrm(ks[6], (DEC_BATCH, D_MODEL)),
        'c_ctx': nrm(ks[7], (D_MODEL,)),
        'w_mod': nrm(ks[8], (DEPTH, D_MODEL, N_MOD * D_MODEL), D_MODEL ** -0.5),
        'b_mod': nrm(ks[9], (DEPTH, N_MOD * D_MODEL), 0.02),
        'ln_g': 1.0 + nrm(ks[10], (DEPTH, 3, D_MODEL), 0.02),
        'ln_b': nrm(ks[11], (DEPTH, 3, D_MODEL), 0.02),
        'ffn1_w1': nrm(ks[12], (DEPTH, D_MODEL, D_FF), D_MODEL ** -0.5),
        'ffn1_w3': nrm(ks[13], (DEPTH, D_MODEL, D_FF), D_MODEL ** -0.5),
        'ffn1_w2': nrm(ks[14], (DEPTH, D_FF, D_MODEL), BETA * D_FF ** -0.5),
        'ffn2_w1': nrm(ks[15], (DEPTH, D_MODEL, D_FF), D_MODEL ** -0.5),
        'ffn2_w3': nrm(ks[16], (DEPTH, D_MODEL, D_FF), D_MODEL ** -0.5),
        'ffn2_w2': nrm(ks[17], (DEPTH, D_FF, D_MODEL), BETA * D_FF ** -0.5),
        'w_in': nrm(ks[18], (DEPTH, D_MODEL, D_IN), D_MODEL ** -0.5),
        'pool_w': nrm(ks[19], (DEPTH, POOL_GROUPS, POOL_GROUP_W, POOL_GROUP_W), POOL_GROUP_W ** -0.5),
        'pool_scale': 1.0 + nrm(ks[20], (DEPTH, POOL_WIDTH), 0.02),
        'w_pa': nrm(ks[21], (DEPTH, POOL_WIDTH, D_MODEL), POOL_WIDTH ** -0.5),
        'w_pb': nrm(ks[22], (DEPTH, DIFF_V_W, D_MODEL), DIFF_V_W ** -0.5),
        'w_pc': nrm(ks[23], (DEPTH, NA_W, D_MODEL), NA_W ** -0.5),
        'lam_q1': nrm(ks[24], (DEPTH, HEAD_DIM), 0.1),
        'lam_k1': nrm(ks[25], (DEPTH, HEAD_DIM), 0.1),
        'lam_q2': nrm(ks[26], (DEPTH, HEAD_DIM), 0.1),
        'lam_k2': nrm(ks[27], (DEPTH, HEAD_DIM), 0.1),
        'subln_g': 1.0 + nrm(ks[28], (DEPTH, 2 * HEAD_DIM), 0.02),
        'na_rpb': nrm(ks[29], (DEPTH, NA_HEADS, 2 * NA_ROWS - 1, 2 * NA_COLS - 1), 0.1),
        'w_out': nrm(ks[30], (DEPTH, D_MODEL, D_MODEL), BETA * D_MODEL ** -0.5),
    }


def reference(x_prompt, x_sample, cache_diff_k, cache_diff_v, cache_na_k, cache_na_v, c, c_ctx,
              w_mod, b_mod, ln_g, ln_b, ffn1_w1, ffn1_w3, ffn1_w2, ffn2_w1, ffn2_w3, ffn2_w2,
              w_in, pool_w, pool_scale, w_pa, w_pb, w_pc, lam_q1, lam_k1, lam_q2, lam_k2,
              subln_g, na_rpb, w_out):

    def modulation(cond, l):
        m = jax.nn.silu(cond) @ w_mod[l] + b_mod[l]
        return m.reshape(m.shape[0], 1, N_MOD, D_MODEL)

    def projections(h, l):
        b, n = h.shape[:2]
        a, qb, kb, vb, qc, kc, vc, gt = split_in(h @ w_in[l])
        return (a,
                qb.reshape(b, n, DIFF_HEADS, 2, HEAD_DIM),
                kb.reshape(b, n, DIFF_HEADS, 2, HEAD_DIM),
                vb.reshape(b, n, DIFF_HEADS, 2 * HEAD_DIM),
                qc.reshape(b, n, NA_HEADS, HEAD_DIM),
                kc.reshape(b, n, NA_HEADS, HEAD_DIM),
                vc.reshape(b, n, NA_HEADS, HEAD_DIM),
                gt)

    def diff_lambda(l):
        lam_init = 0.8 - 0.6 * math.exp(-0.3 * l)
        e1 = jnp.exp(jnp.sum(lam_q1[l].astype(jnp.float32) * lam_k1[l].astype(jnp.float32)))
        e2 = jnp.exp(jnp.sum(lam_q2[l].astype(jnp.float32) * lam_k2[l].astype(jnp.float32)))
        return e1 - e2 + lam_init, lam_init

    def merge(y_a, y_b, y_c, gt, l):
        g = jax.nn.sigmoid(gt.astype(jnp.float32)).astype(gt.dtype)
        g = g.reshape(*gt.shape[:-1], N_BRANCH, D_MODEL)
        mixed = (g[..., 0, :] * (y_a @ w_pa[l]) + g[..., 1, :] * (y_b @ w_pb[l])
                 + g[..., 2, :] * (y_c @ w_pc[l]))
        return mixed @ w_out[l]

    def mix_context(h, l):
        a, qb, kb, vb, qc, kc, vc, gt = projections(h, l)
        lam, lam_init = diff_lambda(l)
        y_a = multiscale_pool(a, pool_w[l], pool_scale[l])
        y_b = diff_attention(qb, kb, vb, lam, lam_init, subln_g[l])
        y_c = dense_attention(qc, kc, vc)
        return merge(y_a, y_b, y_c, gt, l), (kb, vb, kc, vc)

    def mix_latent(h, l):
        b, n = h.shape[:2]
        rows = n // GRID_W
        a, qb, kb, vb, qc, kc, vc, gt = projections(h, l)
        lam, lam_init = diff_lambda(l)
        y_a = multiscale_pool(a.reshape(b, rows, GRID_W, POOL_WIDTH), pool_w[l], pool_scale[l])
        y_a = y_a.reshape(b, n, POOL_WIDTH)
        k_all = jnp.concatenate([rope2d(kb), cache_diff_k[:, l]], axis=1)
        v_all = jnp.concatenate([vb, cache_diff_v[:, l]], axis=1)
        y_b = diff_attention(rope2d(qb), k_all, v_all, lam, lam_init, subln_g[l])
        y_c = neighbourhood_attention(qc, kc, vc, cache_na_k[:, l], cache_na_v[:, l], na_rpb[l])
        return merge(y_a, y_b, y_c, gt, l), None

    def block(x, m, l, mixer):
        def mod(z, i):
            return z * (1.0 + m[:, :, 3 * i + 1]) + m[:, :, 3 * i]

        h = swiglu(mod(x, 0), ffn1_w1[l], ffn1_w3[l], ffn1_w2[l])
        x = layer_norm(ALPHA * x + 0.5 * m[:, :, 2] * h, ln_g[l, 0], ln_b[l, 0])
        h, aux = mixer(mod(x, 1), l)
        x = layer_norm(ALPHA * x + m[:, :, 5] * h, ln_g[l, 1], ln_b[l, 1])
        h = swiglu(mod(x, 2), ffn2_w1[l], ffn2_w3[l], ffn2_w2[l])
        x = layer_norm(ALPHA * x + 0.5 * m[:, :, 8] * h, ln_g[l, 2], ln_b[l, 2])
        return x, aux

    x = x_prompt
    dk, dv, nk, nv = [], [], [], []
    for l in range(DEPTH):
        x, (kb, vb, kc, vc) = block(x, modulation(c_ctx[None, :], l), l, mix_context)
        dk.append(kb)
        dv.append(vb)
        nk.append(kc)
        nv.append(vc)
    y_prompt = x
    new_diff_k = jnp.stack(dk, axis=1)
    new_diff_v = jnp.stack(dv, axis=1)
    new_na_k = jnp.stack(nk, axis=1)
    new_na_v = jnp.stack(nv, axis=1)

    x = x_sample
    for l in range(DEPTH):
        x, _ = block(x, modulation(c, l), l, mix_latent)
    y_sample = x

    return (y_prompt, y_sample, new_diff_k, new_diff_v, new_na_k, new_na_v)
```

```python
import functools
import math

import jax
import jax.numpy as jnp
from jax import lax
from jax.experimental import pallas as pl
from jax.experimental.pallas import tpu as pltpu

D_MODEL = 1024
BATCH = 32
SEQ = 256
DEPTH = 2
DEC_BATCH = 2
DEC_SEQ = 2048
PAST_LEN = 512

GRID_W = 64
HEAD_DIM = 64
POOL_WIDTH = D_MODEL // 4
POOL_GROUPS = 4
POOL_GROUP_W = POOL_WIDTH // POOL_GROUPS
POOL_WINDOWS = (2, 4, 8, 16)
DIFF_HEADS = (D_MODEL // 2) // (2 * HEAD_DIM)
DIFF_W = DIFF_HEADS * 2 * HEAD_DIM
NA_HEADS = (D_MODEL // 4) // HEAD_DIM
NA_W = NA_HEADS * HEAD_DIM
NA_ROWS = 8
NA_COLS = 16
N_BRANCH = 3
D_IN = POOL_WIDTH + 3 * DIFF_W + 3 * NA_W + N_BRANCH * D_MODEL
D_FF = ((8 * D_MODEL // 3 + 127) // 128) * 128
N_MOD = 9
ROPE_BASE = 10000.0
LN_EPS = 1e-5
RMS_EPS = 1e-5
ALPHA = (2 * DEPTH) ** 0.25
ATTN_SCALE = HEAD_DIM ** -0.5
NEG_INF = -1e30

OFF_A = 0
OFF_QB = OFF_A + POOL_WIDTH
OFF_KB = OFF_QB + DIFF_W
OFF_VB = OFF_KB + DIFF_W
OFF_QC = OFF_VB + DIFF_W
OFF_KC = OFF_QC + NA_W
OFF_VC = OFF_KC + NA_W
OFF_GT = OFF_VC + NA_W

LANES = 128
MXU_DIM = 256
VMEM_LIMIT = 56 * 1024 * 1024

FF_CHUNK = MXU_DIM
N_FF_CHUNKS = D_FF // FF_CHUNK
TOK_TILE = 512
MIX_TILE = 256
MOD_ROWS = 16
MOD_COLS = 1024
NA_Q_ROWS = 4
NA_K_ROWS = NA_Q_ROWS + NA_ROWS

BF16 = jnp.bfloat16
F32 = jnp.float32


def _dot(a, b):
    return jnp.dot(a, b, preferred_element_type=F32)


def _dot_nt(a, b):
    return lax.dot_general(a, b, (((1,), (1,)), ((), ())), preferred_element_type=F32)


def _sigmoid(x):
    return 1.0 / (1.0 + jnp.exp(-x))


def _layer_norm(y, g, b):
    mu = jnp.mean(y, axis=-1, keepdims=True)
    d = y - mu
    var = jnp.mean(d * d, axis=-1, keepdims=True)
    return d * lax.rsqrt(var + LN_EPS) * g + b


def _split3(x):
    hi = x.astype(BF16)
    r = x - hi.astype(F32)
    mid = r.astype(BF16)
    lo = (r - mid.astype(F32)).astype(BF16)
    return hi, mid, lo


def _resident(shape):
    n = len(shape)
    return pl.BlockSpec(shape, lambda *_: (0,) * n, pipeline_mode=pl.Buffered(1))


def _params(*sem):
    return pltpu.CompilerParams(dimension_semantics=sem, vmem_limit_bytes=VMEM_LIMIT)


def _mod_kernel(cond_ref, w_ref, b_ref, o_ref):
    c = cond_ref[...]
    x = c * _sigmoid(c)
    w = w_ref[0]
    x_hi = x.astype(BF16)
    x_lo = (x - x_hi.astype(F32)).astype(BF16)
    w_hi = w.astype(BF16)
    w_lo = (w - w_hi.astype(F32)).astype(BF16)
    o_ref[0] = _dot(x_hi, w_hi) + _dot(x_lo, w_hi) + _dot(x_hi, w_lo) + b_ref[0]


def _modulation(cond, w_mod, b_mod):
    n_cols = N_MOD * D_MODEL
    return pl.pallas_call(
        _mod_kernel,
        out_shape=jax.ShapeDtypeStruct((DEPTH, MOD_ROWS, n_cols), F32),
        grid=(DEPTH, n_cols // MOD_COLS),
        in_specs=[
            pl.BlockSpec((MOD_ROWS, D_MODEL), lambda l, j: (0, 0)),
            pl.BlockSpec((1, D_MODEL, MOD_COLS), lambda l, j: (l, 0, j)),
            pl.BlockSpec((1, 1, MOD_COLS), lambda l, j: (l, 0, j)),
        ],
        out_specs=pl.BlockSpec((1, MOD_ROWS, MOD_COLS), lambda l, j: (l, 0, j)),
        compiler_params=_params("parallel", "parallel"),
        name="modulation",
    )(cond, w_mod, b_mod.reshape(DEPTH, 1, n_cols))


def _ffn_kernel(x_ref, m_ref, w1_ref, w3_ref, w2_ref, g_ref, b_ref, o_ref, xm_ref, acc_ref, *, mod_idx):
    x = x_ref[...]
    m = m_ref[0]
    shift = m[3 * mod_idx:3 * mod_idx + 1]
    scale = m[3 * mod_idx + 1:3 * mod_idx + 2]
    gate = m[3 * mod_idx + 2:3 * mod_idx + 3]
    xm_ref[...] = (x * (1.0 + scale) + shift).astype(BF16)
    acc_ref[...] = jnp.zeros_like(acc_ref)

    def chunk(c, carry):
        xm = xm_ref[...]
        h1 = _dot(xm, w1_ref[c])
        h3 = _dot(xm, w3_ref[c])
        act = (h1 * _sigmoid(h1)) * h3
        acc_ref[...] += _dot(act.astype(BF16), w2_ref[c])
        return carry

    lax.fori_loop(0, N_FF_CHUNKS, chunk, 0)
    y = ALPHA * x + (0.5 * gate) * acc_ref[...]
    o_ref[...] = _layer_norm(y, g_ref[...], b_ref[...])


def _ffn(x, m, w1c, w3c, w2c, ln_g, ln_b, *, mod_idx, tiles_per_group):
    n_tok = x.shape[0]
    return pl.pallas_call(
        functools.partial(_ffn_kernel, mod_idx=mod_idx),
        out_shape=jax.ShapeDtypeStruct((n_tok, D_MODEL), F32),
        grid=(n_tok // TOK_TILE,),
        in_specs=[
            pl.BlockSpec((TOK_TILE, D_MODEL), lambda i: (i, 0)),
            pl.BlockSpec((1, N_MOD, D_MODEL), lambda i: (i // tiles_per_group, 0, 0)),
            _resident((N_FF_CHUNKS, D_MODEL, FF_CHUNK)),
            _resident((N_FF_CHUNKS, D_MODEL, FF_CHUNK)),
            _resident((N_FF_CHUNKS, FF_CHUNK, D_MODEL)),
            _resident((1, D_MODEL)),
            _resident((1, D_MODEL)),
        ],
        out_specs=pl.BlockSpec((TOK_TILE, D_MODEL), lambda i: (i, 0)),
        scratch_shapes=[pltpu.VMEM((TOK_TILE, D_MODEL), BF16), pltpu.VMEM((TOK_TILE, D_MODEL), F32)],
        compiler_params=_params("parallel"),
        name="ffn",
    )(x, m, w1c, w3c, w2c, ln_g.reshape(1, D_MODEL), ln_b.reshape(1, D_MODEL))


def _rope(x, cos, sin_signed):
    lane = lax.broadcasted_iota(jnp.int32, x.shape, 1)
    first = (lane % (HEAD_DIM // 2)) < (HEAD_DIM // 4)
    partner = jnp.where(first, pltpu.roll(x, LANES - HEAD_DIM // 4, 1), pltpu.roll(x, HEAD_DIM // 4, 1))
    return x * cos + partner * sin_signed


def _inproj_kernel(*refs, rope):
    if rope:
        (x_ref, m_ref, w_ref, cos_ref, sin_ref,
         a_ref, qb_ref, kb_ref, vb_ref, qc_ref, kc_ref, vc_ref, gt_ref) = refs
    else:
        (x_ref, m_ref, w_ref,
         a_ref, qb_ref, kb_ref, vb_ref, qc_ref, kc_ref, vc_ref, gt_ref) = refs
    m = m_ref[0]
    xm = (x_ref[...] * (1.0 + m[4:5]) + m[3:4]).astype(BF16)

    def proj(off, width):
        return _dot(xm, w_ref[:, off:off + width])

    a_ref[...] = proj(OFF_A, POOL_WIDTH)
    for h in range(DIFF_HEADS):
        lo = h * 2 * HEAD_DIM
        q = proj(OFF_QB + lo, 2 * HEAD_DIM)
        k = proj(OFF_KB + lo, 2 * HEAD_DIM)
        if rope:
            q = _rope(q, cos_ref[...], sin_ref[...])
            k = _rope(k, cos_ref[...], sin_ref[...])
        qb_ref[:, lo:lo + 2 * HEAD_DIM] = (q * ATTN_SCALE).astype(qb_ref.dtype)
        kb_ref[:, lo:lo + 2 * HEAD_DIM] = k.astype(kb_ref.dtype)
    vb_ref[...] = proj(OFF_VB, DIFF_W).astype(vb_ref.dtype)
    qc_ref[...] = (proj(OFF_QC, NA_W) * ATTN_SCALE).astype(qc_ref.dtype)
    kc_ref[...] = proj(OFF_KC, NA_W).astype(kc_ref.dtype)
    vc_ref[...] = proj(OFF_VC, NA_W).astype(vc_ref.dtype)
    for j in range(N_BRANCH):
        gt_ref[:, j * D_MODEL:(j + 1) * D_MODEL] = proj(OFF_GT + j * D_MODEL, D_MODEL)


def _inproj(x, m, w_in, rope_tabs, *, tiles_per_group, kv_dtype):
    n_tok = x.shape[0]
    rope = rope_tabs is not None
    tok = lambda width: pl.BlockSpec((TOK_TILE, width), lambda i: (i, 0))
    in_specs = [
        tok(D_MODEL),
        pl.BlockSpec((1, N_MOD, D_MODEL), lambda i: (i // tiles_per_group, 0, 0)),
        _resident((D_MODEL, D_IN)),
    ]
    args = [x, m, w_in]
    if rope:
        tabs_per_seq = DEC_SEQ // TOK_TILE
        in_specs += [pl.BlockSpec((TOK_TILE, LANES), lambda i: (i % tabs_per_seq, 0))] * 2
        args += list(rope_tabs)
    widths = (POOL_WIDTH, DIFF_W, DIFF_W, DIFF_W, NA_W, NA_W, NA_W, N_BRANCH * D_MODEL)
    dtypes = (F32, BF16, kv_dtype, kv_dtype, BF16, kv_dtype, kv_dtype, F32)
    return pl.pallas_call(
        functools.partial(_inproj_kernel, rope=rope),
        out_shape=[jax.ShapeDtypeStruct((n_tok, w), d) for w, d in zip(widths, dtypes)],
        grid=(n_tok // TOK_TILE,),
        in_specs=in_specs,
        out_specs=[tok(w) for w in widths],
        compiler_params=_params("parallel"),
        name="inproj_rope" if rope else "inproj",
    )(*args)


def _diff_lambda(lq1, lk1, lq2, lk2, lam_init):
    e1 = jnp.exp(jnp.sum(lq1 * lk1, axis=-1, keepdims=True))
    e2 = jnp.exp(jnp.sum(lq2 * lk2, axis=-1, keepdims=True))
    return e1 - e2 + lam_init


def _map_masks(q):
    lane = lax.broadcasted_iota(jnp.int32, q.shape, 1)
    zero = jnp.zeros_like(q)
    return jnp.where(lane < HEAD_DIM, q, zero), jnp.where(lane >= HEAD_DIM, q, zero)


def _head_mask(shape, h):
    lane = lax.broadcasted_iota(jnp.int32, shape, 1)
    return (lane >= h * HEAD_DIM) & (lane < (h + 1) * HEAD_DIM)


def _sub_ln(o, g, lam_init):
    return o * lax.rsqrt(jnp.mean(o * o, axis=-1, keepdims=True) + RMS_EPS) * g * (1.0 - lam_init)


def _ctx_attn_kernel(qb_ref, kb_ref, vb_ref, qc_ref, kc_ref, vc_ref,
                     lq1_ref, lk1_ref, lq2_ref, lk2_ref, g_ref, yb_ref, yc_ref, *, lam_init):
    lam = _diff_lambda(lq1_ref[...], lk1_ref[...], lq2_ref[...], lk2_ref[...], lam_init)
    for h in range(DIFF_HEADS):
        sl = slice(h * 2 * HEAD_DIM, (h + 1) * 2 * HEAD_DIM)
        q0, q1 = _map_masks(qb_ref[:, sl])
        k = kb_ref[:, sl].astype(BF16)
        v = vb_ref[:, sl].astype(BF16)
        probs = []
        for q in (q0, q1):
            s = _dot_nt(q, k)
            e = jnp.exp(s - jnp.max(s, axis=-1, keepdims=True))
            probs.append(e * (1.0 / jnp.sum(e, axis=-1, keepdims=True)))
        a = (probs[0] - lam * probs[1]).astype(BF16)
        yb_ref[:, sl] = _sub_ln(_dot(a, v), g_ref[...], lam_init).astype(yb_ref.dtype)

    qc = qc_ref[...]
    kc = kc_ref[...].astype(BF16)
    vc = vc_ref[...].astype(BF16)
    out = jnp.zeros(qc.shape, F32)
    for h in range(NA_HEADS):
        mask = _head_mask(qc.shape, h)
        s = _dot_nt(jnp.where(mask, qc, jnp.zeros_like(qc)), kc)
        e = jnp.exp(s - jnp.max(s, axis=-1, keepdims=True))
        p = (e * (1.0 / jnp.sum(e, axis=-1, keepdims=True))).astype(BF16)
        out = out + jnp.where(mask, _dot(p, vc), 0.0)
    yc_ref[...] = out.astype(yc_ref.dtype)


def _ctx_attention(qb, kb, vb, qc, kc, vc, lq1, lk1, lq2, lk2, subln_g, *, lam_init):
    n_tok = qb.shape[0]
    seq = lambda width: pl.BlockSpec((SEQ, width), lambda b: (b, 0))
    vec = lambda width: _resident((1, width))
    return pl.pallas_call(
        functools.partial(_ctx_attn_kernel, lam_init=lam_init),
        out_shape=[jax.ShapeDtypeStruct((n_tok, DIFF_W), BF16), jax.ShapeDtypeStruct((n_tok, NA_W), BF16)],
        grid=(n_tok // SEQ,),
        in_specs=[seq(DIFF_W), seq(DIFF_W), seq(DIFF_W), seq(NA_W), seq(NA_W), seq(NA_W),
                  vec(HEAD_DIM), vec(HEAD_DIM), vec(HEAD_DIM), vec(HEAD_DIM), vec(2 * HEAD_DIM)],
        out_specs=[seq(DIFF_W), seq(NA_W)],
        compiler_params=_params("parallel"),
        name="ctx_attention",
    )(qb, kb, vb, qc, kc, vc, lq1, lk1, lq2, lk2, subln_g)


def _lat_diff_kernel(q_ref, k_ref, v_ref, ck_ref, cv_ref,
                     lq1_ref, lk1_ref, lq2_ref, lk2_ref, g_ref, y_ref, *, lam_init):
    lam = _diff_lambda(lq1_ref[...], lk1_ref[...], lq2_ref[...], lk2_ref[...], lam_init)
    for h in range(DIFF_HEADS):
        sl = slice(h * 2 * HEAD_DIM, (h + 1) * 2 * HEAD_DIM)
        q0, q1 = _map_masks(q_ref[0, :, sl])
        k = k_ref[0, :, sl]
        v = v_ref[0, :, sl]
        ck = ck_ref[0, 0, :, sl].astype(BF16)
        cv = cv_ref[0, 0, :, sl].astype(BF16)
        parts = []
        for q in (q0, q1):
            s_new = _dot_nt(q, k)
            s_old = _dot_nt(q, ck)
            top = jnp.maximum(jnp.max(s_new, axis=-1, keepdims=True), jnp.max(s_old, axis=-1, keepdims=True))
            e_new = jnp.exp(s_new - top)
            e_old = jnp.exp(s_old - top)
            inv = 1.0 / (jnp.sum(e_new, axis=-1, keepdims=True) + jnp.sum(e_old, axis=-1, keepdims=True))
            parts.append((e_new, e_old, inv))
        (n0, o0, r0), (n1, o1, r1) = parts
        r1 = lam * r1
        a_new = (n0 * r0 - n1 * r1).astype(BF16)
        a_old = (o0 * r0 - o1 * r1).astype(BF16)
        o = _dot(a_new, v) + _dot(a_old, cv)
        y_ref[0, :, sl] = _sub_ln(o, g_ref[...], lam_init).astype(y_ref.dtype)


def _lat_diff_attention(qb, kb, vb, cache_k, cache_v, layer, lq1, lk1, lq2, lk2, subln_g, *, lam_init):
    vec = lambda width: _resident((1, width))
    full = pl.BlockSpec((1, DEC_SEQ, DIFF_W), lambda b, i: (b, 0, 0))
    past = pl.BlockSpec((1, 1, PAST_LEN, DIFF_W), lambda b, i: (b, layer, 0, 0))
    tile = pl.BlockSpec((1, MIX_TILE, DIFF_W), lambda b, i: (b, i, 0))
    return pl.pallas_call(
        functools.partial(_lat_diff_kernel, lam_init=lam_init),
        out_shape=jax.ShapeDtypeStruct((DEC_BATCH, DEC_SEQ, DIFF_W), BF16),
        grid=(DEC_BATCH, DEC_SEQ // MIX_TILE),
        in_specs=[tile, full, full, past, past,
                  vec(HEAD_DIM), vec(HEAD_DIM), vec(HEAD_DIM), vec(HEAD_DIM), vec(2 * HEAD_DIM)],
        out_specs=tile,
        compiler_params=_params("parallel", "parallel"),
        name="latent_diff_attention",
    )(qb, kb, vb, cache_k, cache_v, lq1, lk1, lq2, lk2, subln_g)


NA_GRID_ROWS = DEC_SEQ // GRID_W
NA_TILES = NA_GRID_ROWS // NA_Q_ROWS
NA_CASE_TILES = (0, 1, NA_TILES - 1)


def _na_window_start(tile):
    return jnp.clip(tile * NA_Q_ROWS - NA_ROWS // 2, 0, NA_GRID_ROWS - NA_K_ROWS)


def _na_bias_table(rpb):
    rows = NA_GRID_ROWS
    col = jnp.arange(GRID_W)
    col_start = jnp.clip(col - NA_COLS // 2, 0, GRID_W - NA_COLS)
    col_ok = (col[None, :] >= col_start[:, None]) & (col[None, :] < col_start[:, None] + NA_COLS)
    dc = jnp.clip(col[None, :] - col[:, None], -(NA_COLS - 1), NA_COLS - 1) + (NA_COLS - 1)
    tables = []
    for tile in NA_CASE_TILES:
        qr = tile * NA_Q_ROWS + jnp.arange(NA_Q_ROWS)
        kr = _na_window_start(tile) + jnp.arange(NA_K_ROWS)
        row_start = jnp.clip(qr - NA_ROWS // 2, 0, rows - NA_ROWS)
        row_ok = (kr[None, :] >= row_start[:, None]) & (kr[None, :] < row_start[:, None] + NA_ROWS)
        dr = jnp.clip(kr[None, :] - qr[:, None] + (NA_ROWS - 1), 0, 2 * NA_ROWS - 2)
        bias = rpb[:, dr][:, :, :, dc]
        ok = row_ok[:, :, None, None] & col_ok[None, None]
        bias = jnp.where(ok[None], bias, NEG_INF)
        bias = bias.transpose(0, 1, 3, 2, 4)
        tables.append(bias.reshape(NA_HEADS, NA_Q_ROWS * GRID_W, NA_K_ROWS * GRID_W))
    return jnp.stack(tables).astype(F32)


def _lat_na_kernel(q_ref, k_ref, v_ref, ck_ref, cv_ref, bias_ref, y_ref):
    start = _na_window_start(pl.program_id(1))
    win = pl.ds(pl.multiple_of(start * GRID_W, NA_Q_ROWS * GRID_W), NA_K_ROWS * GRID_W)
    k = k_ref[0, win, :]
    v = v_ref[0, win, :]
    ck = ck_ref[0, 0].astype(BF16)
    cv = cv_ref[0, 0].astype(BF16)
    q = q_ref[0]
    out = jnp.zeros(q.shape, F32)
    for h in range(NA_HEADS):
        mask = _head_mask(q.shape, h)
        qh = jnp.where(mask, q, jnp.zeros_like(q))
        s_nb = _dot_nt(qh, k) + bias_ref[0, h]
        s_ctx = _dot_nt(qh, ck)
        top = jnp.maximum(jnp.max(s_nb, axis=-1, keepdims=True), jnp.max(s_ctx, axis=-1, keepdims=True))
        e_nb = jnp.exp(s_nb - top)
        e_ctx = jnp.exp(s_ctx - top)
        inv = 1.0 / (jnp.sum(e_nb, axis=-1, keepdims=True) + jnp.sum(e_ctx, axis=-1, keepdims=True))
        o = _dot((e_nb * inv).astype(BF16), v) + _dot((e_ctx * inv).astype(BF16), cv)
        out = out + jnp.where(mask, o, 0.0)
    y_ref[0] = out.astype(y_ref.dtype)


def _lat_na_attention(qc, kc, vc, cache_k, cache_v, layer, bias):
    q_tok = NA_Q_ROWS * GRID_W

    def bias_index(b, i):
        return (jnp.where(i == 0, 0, jnp.where(i == NA_TILES - 1, 2, 1)), 0, 0, 0)

    full = pl.BlockSpec((1, DEC_SEQ, NA_W), lambda b, i: (b, 0, 0))
    past = pl.BlockSpec((1, 1, PAST_LEN, NA_W), lambda b, i: (b, layer, 0, 0))
    tile = pl.BlockSpec((1, q_tok, NA_W), lambda b, i: (b, i, 0))
    return pl.pallas_call(
        _lat_na_kernel,
        out_shape=jax.ShapeDtypeStruct((DEC_BATCH, DEC_SEQ, NA_W), BF16),
        grid=(DEC_BATCH, NA_TILES),
        in_specs=[tile, full, full, past, past,
                  pl.BlockSpec((1, NA_HEADS, q_tok, NA_K_ROWS * GRID_W), bias_index)],
        out_specs=tile,
        compiler_params=_params("parallel", "parallel"),
        name="latent_na_attention",
    )(qc, kc, vc, cache_k, cache_v, bias)


def _pool(a, pool_w_ref, pool_scale, seg_len):
    n = a.shape[0]
    t = lax.broadcasted_iota(jnp.int32, (n, n), 0)
    j = lax.broadcasted_iota(jnp.int32, (n, n), 1)
    shift = seg_len.bit_length() - 1
    same_seg = jnp.right_shift(t, shift) == jnp.right_shift(j, shift)
    outs = []
    for g, w in enumerate(POOL_WINDOWS):
        a_g = a[:, g * POOL_GROUP_W:(g + 1) * POOL_GROUP_W]
        band = jnp.where(same_seg & (j >= t - w // 2) & (j < t + w // 2), 1.0, 0.0)
        count = jnp.sum(band, axis=-1, keepdims=True)
        band = band.astype(BF16)
        hi, mid, lo = _split3(a_g)
        total = _dot(band, hi) + _dot(band, mid) + _dot(band, lo)
        p = total / count - a_g
        outs.append(_dot(p.astype(BF16), pool_w_ref[g]))
    return jnp.concatenate(outs, axis=-1) * pool_scale


def _merge_kernel(x_ref, m_ref, a_ref, yb_ref, yc_ref, gt_ref, pool_w_ref, pool_scale_ref,
                  wpa_ref, wpb_ref, wpc_ref, wout_ref, g_ref, b_ref, o_ref, *, seg_len):
    y_a = _pool(a_ref[...], pool_w_ref, pool_scale_ref[...], seg_len)
    branches = (
        _dot(y_a.astype(BF16), wpa_ref[...]),
        _dot(yb_ref[...], wpb_ref[...]),
        _dot(yc_ref[...], wpc_ref[...]),
    )
    mixed = None
    for j, br in enumerate(branches):
        term = _sigmoid(gt_ref[:, j * D_MODEL:(j + 1) * D_MODEL]) * br
        mixed = term if mixed is None else mixed + term
    h = _dot(mixed.astype(BF16), wout_ref[...])
    m = m_ref[0]
    y = ALPHA * x_ref[...] + m[5:6] * h
    o_ref[...] = _layer_norm(y, g_ref[...], b_ref[...])


def _merge(x, m, a, y_b, y_c, gt, pool_w, pool_scale, w_pa, w_pb, w_pc, w_out, ln_g, ln_b,
           *, seg_len, tiles_per_group):
    n_tok = x.shape[0]
    tok = lambda width: pl.BlockSpec((MIX_TILE, width), lambda i: (i, 0))
    return pl.pallas_call(
        functools.partial(_merge_kernel, seg_len=seg_len),
        out_shape=jax.ShapeDtypeStruct((n_tok, D_MODEL), F32),
        grid=(n_tok // MIX_TILE,),
        in_specs=[
            tok(D_MODEL),
            pl.BlockSpec((1, N_MOD, D_MODEL), lambda i: (i // tiles_per_group, 0, 0)),
            tok(POOL_WIDTH), tok(DIFF_W), tok(NA_W), tok(N_BRANCH * D_MODEL),
            _resident((POOL_GROUPS, POOL_GROUP_W, POOL_GROUP_W)),
            _resident((1, POOL_WIDTH)),
            _resident((POOL_WIDTH, D_MODEL)),
            _resident((DIFF_W, D_MODEL)),
            _resident((NA_W, D_MODEL)),
            _resident((D_MODEL, D_MODEL)),
            _resident((1, D_MODEL)),
            _resident((1, D_MODEL)),
        ],
        out_specs=tok(D_MODEL),
        compiler_params=_params("parallel"),
        name="merge",
    )(x, m, a, y_b, y_c, gt, pool_w, pool_scale, w_pa, w_pb, w_pc, w_out,
      ln_g.reshape(1, D_MODEL), ln_b.reshape(1, D_MODEL))


def _rope_tables():
    t = jnp.arange(DEC_SEQ)
    row = (t // GRID_W).astype(F32)
    col = (t % GRID_W).astype(F32)
    nf = HEAD_DIM // 4
    inv = ROPE_BASE ** (-jnp.arange(nf, dtype=F32) / nf)
    ang_row = row[:, None] * inv
    ang_col = col[:, None] * inv

    def half(ang):
        return (jnp.concatenate([jnp.cos(ang), jnp.cos(ang)], axis=-1),
                jnp.concatenate([-jnp.sin(ang), jnp.sin(ang)], axis=-1))

    cr, sr = half(ang_row)
    cc, sc = half(ang_col)
    cos = jnp.concatenate([cr, cc, cr, cc], axis=-1)
    sin = jnp.concatenate([sr, sc, sr, sc], axis=-1)
    return cos, sin


def _chunk_cols(w):
    return w.reshape(w.shape[0], N_FF_CHUNKS, FF_CHUNK).transpose(1, 0, 2).astype(BF16)


def _chunk_rows(w):
    return w.reshape(N_FF_CHUNKS, FF_CHUNK, w.shape[1]).astype(BF16)


def kernel(x_prompt, x_sample, cache_diff_k, cache_diff_v, cache_na_k, cache_na_v, c, c_ctx, w_mod, b_mod, ln_g, ln_b, ffn1_w1, ffn1_w3, ffn1_w2, ffn2_w1, ffn2_w3, ffn2_w2, w_in, pool_w, pool_scale, w_pa, w_pb, w_pc, lam_q1, lam_k1, lam_q2, lam_k2, subln_g, na_rpb, w_out):
    cond = jnp.concatenate(
        [c_ctx[None, :], c, jnp.zeros((MOD_ROWS - 1 - DEC_BATCH, D_MODEL), F32)], axis=0)
    mod = _modulation(cond, w_mod, b_mod)
    mod = mod[:, :1 + DEC_BATCH].reshape(DEPTH, 1 + DEC_BATCH, N_MOD, D_MODEL)

    cache_dk = cache_diff_k.reshape(DEC_BATCH, DEPTH, PAST_LEN, DIFF_W)
    cache_dv = cache_diff_v.reshape(DEC_BATCH, DEPTH, PAST_LEN, DIFF_W)
    cache_nk = cache_na_k.reshape(DEC_BATCH, DEPTH, PAST_LEN, NA_W)
    cache_nv = cache_na_v.reshape(DEC_BATCH, DEPTH, PAST_LEN, NA_W)
    rope_tabs = _rope_tables()

    x_ctx = x_prompt.reshape(BATCH * SEQ, D_MODEL)
    x_lat = x_sample.reshape(DEC_BATCH * DEC_SEQ, D_MODEL)
    ctx_tiles = (BATCH * SEQ) // TOK_TILE
    lat_tiles = DEC_SEQ // TOK_TILE
    new_dk, new_dv, new_nk, new_nv = [], [], [], []

    for l in range(DEPTH):
        lam_init = 0.8 - 0.6 * math.exp(-0.3 * l)
        m_ctx = mod[l, :1]
        m_lat = mod[l, 1:]
        f1 = (_chunk_cols(ffn1_w1[l]), _chunk_cols(ffn1_w3[l]), _chunk_rows(ffn1_w2[l]))
        f2 = (_chunk_cols(ffn2_w1[l]), _chunk_cols(ffn2_w3[l]), _chunk_rows(ffn2_w2[l]))
        w_in_l = w_in[l].astype(BF16)
        merge_w = (pool_w[l].astype(BF16), pool_scale[l].reshape(1, POOL_WIDTH), w_pa[l].astype(BF16),
                   w_pb[l].astype(BF16), w_pc[l].astype(BF16), w_out[l].astype(BF16))
        lam_vecs = (lam_q1[l].reshape(1, HEAD_DIM), lam_k1[l].reshape(1, HEAD_DIM),
                    lam_q2[l].reshape(1, HEAD_DIM), lam_k2[l].reshape(1, HEAD_DIM),
                    subln_g[l].reshape(1, 2 * HEAD_DIM))

        x_ctx = _ffn(x_ctx, m_ctx, *f1, ln_g[l, 0], ln_b[l, 0], mod_idx=0, tiles_per_group=ctx_tiles)
        a, qb, kb, vb, qc, kc, vc, gt = _inproj(x_ctx, m_ctx, w_in_l, None,
                                                 tiles_per_group=ctx_tiles, kv_dtype=F32)
        y_b, y_c = _ctx_attention(qb, kb, vb, qc, kc, vc, *lam_vecs, lam_init=lam_init)
        x_ctx = _merge(x_ctx, m_ctx, a, y_b, y_c, gt, *merge_w, ln_g[l, 1], ln_b[l, 1],
                       seg_len=SEQ, tiles_per_group=(BATCH * SEQ) // MIX_TILE)
        x_ctx = _ffn(x_ctx, m_ctx, *f2, ln_g[l, 2], ln_b[l, 2], mod_idx=2, tiles_per_group=ctx_tiles)
        new_dk.append(kb.reshape(BATCH, SEQ, DIFF_HEADS, 2, HEAD_DIM))
        new_dv.append(vb.reshape(BATCH, SEQ, DIFF_HEADS, 2 * HEAD_DIM))
        new_nk.append(kc.reshape(BATCH, SEQ, NA_HEADS, HEAD_DIM))
        new_nv.append(vc.reshape(BATCH, SEQ, NA_HEADS, HEAD_DIM))

        x_lat = _ffn(x_lat, m_lat, *f1, ln_g[l, 0], ln_b[l, 0], mod_idx=0, tiles_per_group=lat_tiles)
        a, qb, kb, vb, qc, kc, vc, gt = _inproj(x_lat, m_lat, w_in_l, rope_tabs,
                                                 tiles_per_group=lat_tiles, kv_dtype=BF16)
        seq3 = lambda z: z.reshape(DEC_BATCH, DEC_SEQ, z.shape[-1])
        y_b = _lat_diff_attention(seq3(qb), seq3(kb), seq3(vb), cache_dk, cache_dv, l,
                                  *lam_vecs, lam_init=lam_init)
        y_c = _lat_na_attention(seq3(qc), seq3(kc), seq3(vc), cache_nk, cache_nv, l,
                                _na_bias_table(na_rpb[l]))
        x_lat = _merge(x_lat, m_lat, a, y_b.reshape(-1, DIFF_W), y_c.reshape(-1, NA_W), gt, *merge_w,
                       ln_g[l, 1], ln_b[l, 1], seg_len=GRID_W, tiles_per_group=DEC_SEQ // MIX_TILE)
        x_lat = _ffn(x_lat, m_lat, *f2, ln_g[l, 2], ln_b[l, 2], mod_idx=2, tiles_per_group=lat_tiles)

    return (x_ctx.reshape(BATCH, SEQ, D_MODEL),
            x_lat.reshape(DEC_BATCH, DEC_SEQ, D_MODEL),
            jnp.stack(new_dk, axis=1), jnp.stack(new_dv, axis=1),
            jnp.stack(new_nk, axis=1), jnp.stack(new_nv, axis=1))
```

```python
import functools
import math

import jax
import jax.numpy as jnp
from jax import lax
from jax.experimental import pallas as pl
from jax.experimental.pallas import tpu as pltpu

D_MODEL = 1024
BATCH = 32
SEQ = 256
DEPTH = 2
DEC_BATCH = 2
DEC_SEQ = 2048
PAST_LEN = 512

GRID_W = 64
HEAD_DIM = 64
POOL_WIDTH = D_MODEL // 4
POOL_GROUPS = 4
POOL_GROUP_W = POOL_WIDTH // POOL_GROUPS
POOL_WINDOWS = (2, 4, 8, 16)
DIFF_HEADS = (D_MODEL // 2) // (2 * HEAD_DIM)
DIFF_W = DIFF_HEADS * 2 * HEAD_DIM
NA_HEADS = (D_MODEL // 4) // HEAD_DIM
NA_W = NA_HEADS * HEAD_DIM
NA_ROWS = 8
NA_COLS = 16
N_BRANCH = 3
D_IN = POOL_WIDTH + 3 * DIFF_W + 3 * NA_W + N_BRANCH * D_MODEL
D_FF = ((8 * D_MODEL // 3 + 127) // 128) * 128
N_MOD = 9
ROPE_BASE = 10000.0
LN_EPS = 1e-5
RMS_EPS = 1e-5
ALPHA = (2 * DEPTH) ** 0.25
ATTN_SCALE = HEAD_DIM ** -0.5
NEG_INF = -1e30

OFF_A = 0
OFF_QB = OFF_A + POOL_WIDTH
OFF_KB = OFF_QB + DIFF_W
OFF_VB = OFF_KB + DIFF_W
OFF_QC = OFF_VB + DIFF_W
OFF_KC = OFF_QC + NA_W
OFF_VC = OFF_KC + NA_W
OFF_GT = OFF_VC + NA_W
KT_W = DIFF_W + 2 * NA_W

LANES = 128
MXU_DIM = 256
VMEM_LIMIT = 56 * 1024 * 1024

FF_CHUNK = MXU_DIM
N_FF_CHUNKS = D_FF // FF_CHUNK
TOK_TILE = 512
MIX_TILE = 256
MOD_ROWS = 16
MOD_COLS = 1024
NA_Q_ROWS = 4
NA_K_ROWS = NA_Q_ROWS + NA_ROWS
NA_GRID_ROWS = DEC_SEQ // GRID_W
NA_TILES = NA_GRID_ROWS // NA_Q_ROWS
NA_CASE_TILES = (0, 1, NA_TILES - 1)
SEQ_PER_TILE = TOK_TILE // SEQ

BF16 = jnp.bfloat16
F32 = jnp.float32


def _dot(a, b):
    return jnp.dot(a, b, preferred_element_type=F32)


def _dot_nt(a, b):
    return lax.dot_general(a, b, (((1,), (1,)), ((), ())), preferred_element_type=F32)


def _sigmoid(x):
    return 1.0 / (1.0 + jnp.exp(-x))


def _layer_norm(y, g, b):
    mu = jnp.mean(y, axis=-1, keepdims=True)
    d = y - mu
    var = jnp.mean(d * d, axis=-1, keepdims=True)
    return d * lax.rsqrt(var + LN_EPS) * g + b


def _split3(x):
    hi = x.astype(BF16)
    r = x - hi.astype(F32)
    mid = r.astype(BF16)
    lo = (r - mid.astype(F32)).astype(BF16)
    return hi, mid, lo


def _resident(shape, index=None):
    index = (0,) * len(shape) if index is None else index
    return pl.BlockSpec(shape, lambda *_: index, pipeline_mode=pl.Buffered(1))


def _layer_block(shape, layer):
    return _resident((1,) + shape, (layer,) + (0,) * len(shape))


def _params(*sem):
    return pltpu.CompilerParams(dimension_semantics=sem, vmem_limit_bytes=VMEM_LIMIT)


def _mod_kernel(cond_ref, w_ref, b_ref, o_ref):
    c = cond_ref[...]
    x = c * _sigmoid(c)
    w = w_ref[0]
    x_hi = x.astype(BF16)
    x_lo = (x - x_hi.astype(F32)).astype(BF16)
    w_hi = w.astype(BF16)
    w_lo = (w - w_hi.astype(F32)).astype(BF16)
    o_ref[0] = _dot(x_hi, w_hi) + _dot(x_lo, w_hi) + _dot(x_hi, w_lo) + b_ref[0]


def _modulation(cond, w_mod, b_mod):
    n_cols = N_MOD * D_MODEL
    return pl.pallas_call(
        _mod_kernel,
        out_shape=jax.ShapeDtypeStruct((DEPTH, MOD_ROWS, n_cols), F32),
        grid=(DEPTH, n_cols // MOD_COLS),
        in_specs=[
            pl.BlockSpec((MOD_ROWS, D_MODEL), lambda l, j: (0, 0)),
            pl.BlockSpec((1, D_MODEL, MOD_COLS), lambda l, j: (l, 0, j)),
            pl.BlockSpec((1, 1, MOD_COLS), lambda l, j: (l, 0, j)),
        ],
        out_specs=pl.BlockSpec((1, MOD_ROWS, MOD_COLS), lambda l, j: (l, 0, j)),
        compiler_params=_params("parallel", "parallel"),
        name="modulation",
    )(cond, w_mod, b_mod.reshape(DEPTH, 1, n_cols))


def _ffn_kernel(x_ref, m_ref, w1_ref, w3_ref, w2_ref, g_ref, b_ref, o_ref, xm_ref, acc_ref, *, mod_idx, ln_idx):
    x = x_ref[...]
    m = m_ref[0, 0]
    shift = m[3 * mod_idx:3 * mod_idx + 1]
    scale = m[3 * mod_idx + 1:3 * mod_idx + 2]
    gate = m[3 * mod_idx + 2:3 * mod_idx + 3]
    xm_ref[...] = (x * (1.0 + scale) + shift).astype(BF16)

    def chunk_out(c):
        cols = pl.ds(pl.multiple_of(c * FF_CHUNK, FF_CHUNK), FF_CHUNK)
        xm = xm_ref[...]
        h1 = _dot(xm, w1_ref[0, :, cols])
        h3 = _dot(xm, w3_ref[0, :, cols])
        act = (h1 * _sigmoid(h1)) * h3
        return _dot(act.astype(BF16), w2_ref[0, cols, :])

    acc_ref[...] = chunk_out(0)

    def chunk(c, carry):
        acc_ref[...] += chunk_out(c)
        return carry

    lax.fori_loop(1, N_FF_CHUNKS, chunk, 0)
    y = ALPHA * x + (0.5 * gate) * acc_ref[...]
    o_ref[...] = _layer_norm(y, g_ref[0, ln_idx:ln_idx + 1], b_ref[0, ln_idx:ln_idx + 1])


def _mod_spec(layer, tiles_per_group):
    return pl.BlockSpec((1, 1, N_MOD, D_MODEL), lambda i: (layer, i // tiles_per_group, 0, 0))


def _ffn(x, mod, w1, w3, w2, ln_g, ln_b, *, layer, mod_idx, ln_idx, tiles_per_group):
    n_tok = x.shape[0]
    return pl.pallas_call(
        functools.partial(_ffn_kernel, mod_idx=mod_idx, ln_idx=ln_idx),
        out_shape=jax.ShapeDtypeStruct((n_tok, D_MODEL), F32),
        grid=(n_tok // TOK_TILE,),
        in_specs=[
            pl.BlockSpec((TOK_TILE, D_MODEL), lambda i: (i, 0)),
            _mod_spec(layer, tiles_per_group),
            _layer_block((D_MODEL, D_FF), layer),
            _layer_block((D_MODEL, D_FF), layer),
            _layer_block((D_FF, D_MODEL), layer),
            _layer_block((3, D_MODEL), layer),
            _layer_block((3, D_MODEL), layer),
        ],
        out_specs=pl.BlockSpec((TOK_TILE, D_MODEL), lambda i: (i, 0)),
        scratch_shapes=[pltpu.VMEM((TOK_TILE, D_MODEL), BF16), pltpu.VMEM((TOK_TILE, D_MODEL), F32)],
        compiler_params=_params("parallel"),
        name="ffn",
    )(x, mod, w1, w3, w2, ln_g, ln_b)


def _rope(x, cos, sin_signed):
    lane = lax.broadcasted_iota(jnp.int32, x.shape, 1)
    first = (lane % (HEAD_DIM // 2)) < (HEAD_DIM // 4)
    partner = jnp.where(first, pltpu.roll(x, LANES - HEAD_DIM // 4, 1), pltpu.roll(x, HEAD_DIM // 4, 1))
    return x * cos + partner * sin_signed


def _modulated(x_ref, m_ref):
    m = m_ref[0, 0]
    return (x_ref[...] * (1.0 + m[4:5]) + m[3:4]).astype(BF16)


def _inproj_ctx_kernel(*refs):
    x_ref, m_ref, w_ref, wt_ref = refs[:4]
    a_ref, qb_ref, kbt_ref, vb_ref, qc_ref, kct_ref, vct_ref, gt_ref = refs[-8:]
    xm = _modulated(x_ref, m_ref)

    def proj(off, width):
        return _dot(xm, w_ref[0, :, off:off + width])

    a_ref[...] = proj(OFF_A, POOL_WIDTH)
    qb_ref[...] = (proj(OFF_QB, DIFF_W) * ATTN_SCALE).astype(qb_ref.dtype)
    qc_ref[...] = (proj(OFF_QC, NA_W) * ATTN_SCALE).astype(qc_ref.dtype)
    v = proj(OFF_VB, DIFF_W)
    for j in range(N_BRANCH):
        gt_ref[:, j * D_MODEL:(j + 1) * D_MODEL] = proj(OFF_GT + j * D_MODEL, D_MODEL)
    zt = _dot_nt(wt_ref[0], xm)
    for s in range(SEQ_PER_TILE):
        tok = slice(s * SEQ, (s + 1) * SEQ)
        vb_ref[s, 0] = v[tok]
        kbt_ref[s, 0] = zt[:DIFF_W, tok]
        kct_ref[s, 0] = zt[DIFF_W:DIFF_W + NA_W, tok]
        vct_ref[s, 0] = zt[DIFF_W + NA_W:, tok]


def _inproj_ctx(x, mod, w_in, w_in_t, kv_prev, *, layer):
    n_tok = x.shape[0]
    tok = lambda width: pl.BlockSpec((TOK_TILE, width), lambda i: (i, 0))
    kv = lambda rows, cols: pl.BlockSpec((SEQ_PER_TILE, 1, rows, cols), lambda i: (i, layer, 0, 0))
    kv_shape = lambda rows, cols: jax.ShapeDtypeStruct((BATCH, DEPTH, rows, cols), F32)
    in_specs = [
        tok(D_MODEL),
        _mod_spec(layer, n_tok // TOK_TILE),
        _layer_block((D_MODEL, D_IN), layer),
        _layer_block((KT_W, D_MODEL), layer),
    ]
    args = [x, mod, w_in, w_in_t]
    aliases = {}
    if kv_prev is not None:
        for arr, out_idx in zip(kv_prev, (2, 3, 5, 6)):
            aliases[len(args)] = out_idx
            args.append(arr)
            in_specs.append(pl.BlockSpec(memory_space=pl.ANY))
    return pl.pallas_call(
        _inproj_ctx_kernel,
        out_shape=[
            jax.ShapeDtypeStruct((n_tok, POOL_WIDTH), F32),
            jax.ShapeDtypeStruct((n_tok, DIFF_W), BF16),
            kv_shape(DIFF_W, SEQ),
            kv_shape(SEQ, DIFF_W),
            jax.ShapeDtypeStruct((n_tok, NA_W), BF16),
            kv_shape(NA_W, SEQ),
            kv_shape(NA_W, SEQ),
            jax.ShapeDtypeStruct((n_tok, N_BRANCH * D_MODEL), F32),
        ],
        grid=(n_tok // TOK_TILE,),
        in_specs=in_specs,
        out_specs=[tok(POOL_WIDTH), tok(DIFF_W), kv(DIFF_W, SEQ), kv(SEQ, DIFF_W),
                   tok(NA_W), kv(NA_W, SEQ), kv(NA_W, SEQ), tok(N_BRANCH * D_MODEL)],
        input_output_aliases=aliases,
        compiler_params=_params("parallel"),
        name="inproj_ctx",
    )(*args)


def _inproj_lat_kernel(x_ref, m_ref, w_ref, cos_ref, sin_ref,
                       a_ref, qb_ref, kb_ref, vb_ref, qc_ref, kc_ref, vc_ref, gt_ref):
    xm = _modulated(x_ref, m_ref)

    def proj(off, width):
        return _dot(xm, w_ref[0, :, off:off + width])

    a_ref[...] = proj(OFF_A, POOL_WIDTH)
    for h in range(DIFF_HEADS):
        lo = h * 2 * HEAD_DIM
        q = _rope(proj(OFF_QB + lo, 2 * HEAD_DIM), cos_ref[...], sin_ref[...])
        k = _rope(proj(OFF_KB + lo, 2 * HEAD_DIM), cos_ref[...], sin_ref[...])
        qb_ref[:, lo:lo + 2 * HEAD_DIM] = (q * ATTN_SCALE).astype(qb_ref.dtype)
        kb_ref[:, lo:lo + 2 * HEAD_DIM] = k.astype(kb_ref.dtype)
    vb_ref[...] = proj(OFF_VB, DIFF_W).astype(vb_ref.dtype)
    qc_ref[...] = (proj(OFF_QC, NA_W) * ATTN_SCALE).astype(qc_ref.dtype)
    kc_ref[...] = proj(OFF_KC, NA_W).astype(kc_ref.dtype)
    vc_ref[...] = proj(OFF_VC, NA_W).astype(vc_ref.dtype)
    for j in range(N_BRANCH):
        gt_ref[:, j * D_MODEL:(j + 1) * D_MODEL] = proj(OFF_GT + j * D_MODEL, D_MODEL)


def _inproj_lat(x, mod, w_in, rope_tabs, *, layer):
    n_tok = x.shape[0]
    tok = lambda width: pl.BlockSpec((TOK_TILE, width), lambda i: (i, 0))
    tiles_per_seq = DEC_SEQ // TOK_TILE
    widths = (POOL_WIDTH, DIFF_W, DIFF_W, DIFF_W, NA_W, NA_W, NA_W, N_BRANCH * D_MODEL)
    dtypes = (F32, BF16, BF16, BF16, BF16, BF16, BF16, F32)
    return pl.pallas_call(
        _inproj_lat_kernel,
        out_shape=[jax.ShapeDtypeStruct((n_tok, w), d) for w, d in zip(widths, dtypes)],
        grid=(n_tok // TOK_TILE,),
        in_specs=[
            tok(D_MODEL),
            _mod_spec(layer, tiles_per_seq),
            _layer_block((D_MODEL, D_IN), layer),
            pl.BlockSpec((TOK_TILE, LANES), lambda i: (i % tiles_per_seq, 0)),
            pl.BlockSpec((TOK_TILE, LANES), lambda i: (i % tiles_per_seq, 0)),
        ],
        out_specs=[tok(w) for w in widths],
        compiler_params=_params("parallel"),
        name="inproj_lat",
    )(x, mod, w_in, *rope_tabs)


def _diff_lambda(lq1, lk1, lq2, lk2, lam_init):
    e1 = jnp.exp(jnp.sum(lq1 * lk1, axis=-1, keepdims=True))
    e2 = jnp.exp(jnp.sum(lq2 * lk2, axis=-1, keepdims=True))
    return e1 - e2 + lam_init


def _map_masks(q):
    lane = lax.broadcasted_iota(jnp.int32, q.shape, 1)
    zero = jnp.zeros_like(q)
    return jnp.where(lane < HEAD_DIM, q, zero), jnp.where(lane >= HEAD_DIM, q, zero)


def _head_mask(shape, h):
    lane = lax.broadcasted_iota(jnp.int32, shape, 1)
    return (lane >= h * HEAD_DIM) & (lane < (h + 1) * HEAD_DIM)


def _sub_ln(o, g, lam_init):
    return o * lax.rsqrt(jnp.mean(o * o, axis=-1, keepdims=True) + RMS_EPS) * g * (1.0 - lam_init)


_LAM_SPECS = lambda layer: [_layer_block((1, HEAD_DIM), layer)] * 4 + [_layer_block((1, 2 * HEAD_DIM), layer)]


def _lam_args(lam_q1, lam_k1, lam_q2, lam_k2, subln_g):
    vec = lambda a: a.reshape(DEPTH, 1, a.shape[-1])
    return vec(lam_q1), vec(lam_k1), vec(lam_q2), vec(lam_k2), vec(subln_g)


def _ctx_attn_kernel(qb_ref, kbt_ref, vb_ref, qc_ref, kct_ref, vct_ref,
                     lq1_ref, lk1_ref, lq2_ref, lk2_ref, g_ref, yb_ref, yc_ref, *, lam_init):
    lam = _diff_lambda(lq1_ref[0], lk1_ref[0], lq2_ref[0], lk2_ref[0], lam_init)
    for h in range(DIFF_HEADS):
        sl = slice(h * 2 * HEAD_DIM, (h + 1) * 2 * HEAD_DIM)
        q0, q1 = _map_masks(qb_ref[:, sl])
        kt = kbt_ref[0, 0, sl, :].astype(BF16)
        v = vb_ref[0, 0, :, sl].astype(BF16)
        probs = []
        for q in (q0, q1):
            s = _dot(q, kt)
            e = jnp.exp(s - jnp.max(s, axis=-1, keepdims=True))
            probs.append(e * (1.0 / jnp.sum(e, axis=-1, keepdims=True)))
        a = (probs[0] - lam * probs[1]).astype(BF16)
        yb_ref[:, sl] = _sub_ln(_dot(a, v), g_ref[0], lam_init).astype(yb_ref.dtype)

    qc = qc_ref[...]
    kct = kct_ref[0, 0].astype(BF16)
    vct = vct_ref[0, 0].astype(BF16)
    out = jnp.zeros(qc.shape, F32)
    for h in range(NA_HEADS):
        mask = _head_mask(qc.shape, h)
        s = _dot(jnp.where(mask, qc, jnp.zeros_like(qc)), kct)
        e = jnp.exp(s - jnp.max(s, axis=-1, keepdims=True))
        p = (e * (1.0 / jnp.sum(e, axis=-1, keepdims=True))).astype(BF16)
        out = out + jnp.where(mask, _dot_nt(p, vct), 0.0)
    yc_ref[...] = out.astype(yc_ref.dtype)


def _ctx_attention(qb, kbt, vb, qc, kct, vct, lam_args, *, layer, lam_init):
    n_tok = qb.shape[0]
    seq = lambda width: pl.BlockSpec((SEQ, width), lambda b: (b, 0))
    kv = lambda rows, cols: pl.BlockSpec((1, 1, rows, cols), lambda b: (b, layer, 0, 0))
    return pl.pallas_call(
        functools.partial(_ctx_attn_kernel, lam_init=lam_init),
        out_shape=[jax.ShapeDtypeStruct((n_tok, DIFF_W), BF16), jax.ShapeDtypeStruct((n_tok, NA_W), BF16)],
        grid=(n_tok // SEQ,),
        in_specs=[seq(DIFF_W), kv(DIFF_W, SEQ), kv(SEQ, DIFF_W), seq(NA_W), kv(NA_W, SEQ), kv(NA_W, SEQ)]
                 + _LAM_SPECS(layer),
        out_specs=[seq(DIFF_W), seq(NA_W)],
        compiler_params=_params("parallel"),
        name="ctx_attention",
    )(qb, kbt, vb, qc, kct, vct, *lam_args)


def _lat_diff_kernel(q_ref, k_ref, v_ref, ckt_ref, cv_ref,
                     lq1_ref, lk1_ref, lq2_ref, lk2_ref, g_ref, y_ref, *, lam_init):
    lam = _diff_lambda(lq1_ref[0], lk1_ref[0], lq2_ref[0], lk2_ref[0], lam_init)
    for h in range(DIFF_HEADS):
        sl = slice(h * 2 * HEAD_DIM, (h + 1) * 2 * HEAD_DIM)
        q0, q1 = _map_masks(q_ref[0, :, sl])
        k = k_ref[0, :, sl]
        v = v_ref[0, :, sl]
        ckt = ckt_ref[0, 0, sl, :].astype(BF16)
        cv = cv_ref[0, 0, :, sl].astype(BF16)
        parts = []
        for q in (q0, q1):
            s_new = _dot_nt(q, k)
            s_old = _dot(q, ckt)
            top = jnp.maximum(jnp.max(s_new, axis=-1, keepdims=True), jnp.max(s_old, axis=-1, keepdims=True))
            e_new = jnp.exp(s_new - top)
            e_old = jnp.exp(s_old - top)
            inv = 1.0 / (jnp.sum(e_new, axis=-1, keepdims=True) + jnp.sum(e_old, axis=-1, keepdims=True))
            parts.append((e_new, e_old, inv))
        (n0, o0, r0), (n1, o1, r1) = parts
        r1 = lam * r1
        a_new = (n0 * r0 - n1 * r1).astype(BF16)
        a_old = (o0 * r0 - o1 * r1).astype(BF16)
        o = _dot(a_new, v) + _dot(a_old, cv)
        y_ref[0, :, sl] = _sub_ln(o, g_ref[0], lam_init).astype(y_ref.dtype)


def _lat_diff_attention(qb, kb, vb, cache_kt, cache_v, lam_args, *, layer, lam_init):
    full = pl.BlockSpec((1, DEC_SEQ, DIFF_W), lambda b, i: (b, 0, 0))
    past = lambda rows, cols: pl.BlockSpec((1, 1, rows, cols), lambda b, i: (b, layer, 0, 0))
    tile = pl.BlockSpec((1, MIX_TILE, DIFF_W), lambda b, i: (b, i, 0))
    return pl.pallas_call(
        functools.partial(_lat_diff_kernel, lam_init=lam_init),
        out_shape=jax.ShapeDtypeStruct((DEC_BATCH, DEC_SEQ, DIFF_W), BF16),
        grid=(DEC_BATCH, DEC_SEQ // MIX_TILE),
        in_specs=[tile, full, full, past(DIFF_W, PAST_LEN), past(PAST_LEN, DIFF_W)] + _LAM_SPECS(layer),
        out_specs=tile,
        compiler_params=_params("parallel", "parallel"),
        name="latent_diff_attention",
    )(qb, kb, vb, cache_kt, cache_v, *lam_args)


def _na_window_start(tile):
    return jnp.clip(tile * NA_Q_ROWS - NA_ROWS // 2, 0, NA_GRID_ROWS - NA_K_ROWS)


def _toeplitz(v):
    n = GRID_W
    period = 2 * n - 1
    reps = [1] * (v.ndim - 1) + [n + 1]
    flat = jnp.tile(v, reps)[..., :n * 2 * n]
    skew = flat.reshape(*v.shape[:-1], n, 2 * n)
    return skew[..., ::-1, :n]


def _na_bias_table(rpb):
    edge = GRID_W - NA_COLS
    ext = jnp.concatenate([jnp.repeat(rpb[..., :1], edge, axis=-1), rpb,
                           jnp.repeat(rpb[..., -1:], edge, axis=-1)], axis=-1)
    blocks = _toeplitz(ext)
    col = jnp.arange(GRID_W)
    col_start = jnp.clip(col - NA_COLS // 2, 0, GRID_W - NA_COLS)
    col_ok = (col[None, :] >= col_start[:, None]) & (col[None, :] < col_start[:, None] + NA_COLS)
    blocks = jnp.where(col_ok, blocks, NEG_INF)
    masked = jnp.full((NA_HEADS, GRID_W, GRID_W), NEG_INF, F32)
    tables = []
    for tile in NA_CASE_TILES:
        k0 = min(max(tile * NA_Q_ROWS - NA_ROWS // 2, 0), NA_GRID_ROWS - NA_K_ROWS)
        q_rows = []
        for qr in range(tile * NA_Q_ROWS, (tile + 1) * NA_Q_ROWS):
            row_start = min(max(qr - NA_ROWS // 2, 0), NA_GRID_ROWS - NA_ROWS)
            parts = []
            for kr in range(k0, k0 + NA_K_ROWS):
                inside = row_start <= kr < row_start + NA_ROWS
                parts.append(blocks[:, kr - qr + NA_ROWS - 1] if inside else masked)
            q_rows.append(jnp.concatenate(parts, axis=-1))
        tables.append(jnp.concatenate(q_rows, axis=1))
    return jnp.stack(tables).astype(F32)


def _lat_na_kernel(q_ref, k_ref, v_ref, ckt_ref, cvt_ref, bias_ref, y_ref):
    start = _na_window_start(pl.program_id(1))
    win = pl.ds(pl.multiple_of(start * GRID_W, NA_Q_ROWS * GRID_W), NA_K_ROWS * GRID_W)
    k = k_ref[0, win, :]
    v = v_ref[0, win, :]
    ckt = ckt_ref[0, 0].astype(BF16)
    cvt = cvt_ref[0, 0].astype(BF16)
    q = q_ref[0]
    out = jnp.zeros(q.shape, F32)
    for h in range(NA_HEADS):
        mask = _head_mask(q.shape, h)
        qh = jnp.where(mask, q, jnp.zeros_like(q))
        s_nb = _dot_nt(qh, k) + bias_ref[0, h]
        s_ctx = _dot(qh, ckt)
        top = jnp.maximum(jnp.max(s_nb, axis=-1, keepdims=True), jnp.max(s_ctx, axis=-1, keepdims=True))
        e_nb = jnp.exp(s_nb - top)
        e_ctx = jnp.exp(s_ctx - top)
        inv = 1.0 / (jnp.sum(e_nb, axis=-1, keepdims=True) + jnp.sum(e_ctx, axis=-1, keepdims=True))
        o = _dot((e_nb * inv).astype(BF16), v) + _dot_nt((e_ctx * inv).astype(BF16), cvt)
        out = out + jnp.where(mask, o, 0.0)
    y_ref[0] = out.astype(y_ref.dtype)


def _lat_na_attention(qc, kc, vc, cache_kt, cache_vt, bias, *, layer):
    q_tok = NA_Q_ROWS * GRID_W

    def bias_index(b, i):
        return (jnp.where(i == 0, 0, jnp.where(i == NA_TILES - 1, 2, 1)), 0, 0, 0)

    full = pl.BlockSpec((1, DEC_SEQ, NA_W), lambda b, i: (b, 0, 0))
    past = pl.BlockSpec((1, 1, NA_W, PAST_LEN), lambda b, i: (b, layer, 0, 0))
    tile = pl.BlockSpec((1, q_tok, NA_W), lambda b, i: (b, i, 0))
    return pl.pallas_call(
        _lat_na_kernel,
        out_shape=jax.ShapeDtypeStruct((DEC_BATCH, DEC_SEQ, NA_W), BF16),
        grid=(DEC_BATCH, NA_TILES),
        in_specs=[tile, full, full, past, past,
                  pl.BlockSpec((1, NA_HEADS, q_tok, NA_K_ROWS * GRID_W), bias_index)],
        out_specs=tile,
        compiler_params=_params("parallel", "parallel"),
        name="latent_na_attention",
    )(qc, kc, vc, cache_kt, cache_vt, bias)


def _pool(a, pool_w_ref, pool_scale, seg_len):
    n = a.shape[0]
    t = lax.broadcasted_iota(jnp.int32, (n, n), 0)
    j = lax.broadcasted_iota(jnp.int32, (n, n), 1)
    shift = seg_len.bit_length() - 1
    same_seg = jnp.right_shift(t, shift) == jnp.right_shift(j, shift)
    outs = []
    for g, w in enumerate(POOL_WINDOWS):
        a_g = a[:, g * POOL_GROUP_W:(g + 1) * POOL_GROUP_W]
        band = jnp.where(same_seg & (j >= t - w // 2) & (j < t + w // 2), 1.0, 0.0)
        count = jnp.sum(band, axis=-1, keepdims=True)
        band = band.astype(BF16)
        hi, mid, lo = _split3(a_g)
        total = _dot(band, hi) + _dot(band, mid) + _dot(band, lo)
        p = total / count - a_g
        outs.append(_dot(p.astype(BF16), pool_w_ref[0, g]))
    return jnp.concatenate(outs, axis=-1) * pool_scale


def _merge_kernel(x_ref, m_ref, a_ref, yb_ref, yc_ref, gt_ref, pool_w_ref, pool_scale_ref,
                  wpa_ref, wpb_ref, wpc_ref, wout_ref, g_ref, b_ref, o_ref, *, seg_len):
    y_a = _pool(a_ref[...], pool_w_ref, pool_scale_ref[0], seg_len)
    branches = (
        _dot(y_a.astype(BF16), wpa_ref[0]),
        _dot(yb_ref[...], wpb_ref[0]),
        _dot(yc_ref[...], wpc_ref[0]),
    )
    mixed = None
    for j, br in enumerate(branches):
        term = _sigmoid(gt_ref[:, j * D_MODEL:(j + 1) * D_MODEL]) * br
        mixed = term if mixed is None else mixed + term
    h = _dot(mixed.astype(BF16), wout_ref[0])
    m = m_ref[0, 0]
    y = ALPHA * x_ref[...] + m[5:6] * h
    o_ref[...] = _layer_norm(y, g_ref[0, 1:2], b_ref[0, 1:2])


def _merge(x, mod, a, y_b, y_c, gt, pool_w, pool_scale, w_pa, w_pb, w_pc, w_out, ln_g, ln_b,
           *, layer, seg_len, tiles_per_group):
    n_tok = x.shape[0]
    tok = lambda width: pl.BlockSpec((MIX_TILE, width), lambda i: (i, 0))
    return pl.pallas_call(
        functools.partial(_merge_kernel, seg_len=seg_len),
        out_shape=jax.ShapeDtypeStruct((n_tok, D_MODEL), F32),
        grid=(n_tok // MIX_TILE,),
        in_specs=[
            tok(D_MODEL),
            _mod_spec(layer, tiles_per_group),
            tok(POOL_WIDTH), tok(DIFF_W), tok(NA_W), tok(N_BRANCH * D_MODEL),
            _layer_block((POOL_GROUPS, POOL_GROUP_W, POOL_GROUP_W), layer),
            _layer_block((1, POOL_WIDTH), layer),
            _layer_block((POOL_WIDTH, D_MODEL), layer),
            _layer_block((DIFF_W, D_MODEL), layer),
            _layer_block((NA_W, D_MODEL), layer),
            _layer_block((D_MODEL, D_MODEL), layer),
            _layer_block((3, D_MODEL), layer),
            _layer_block((3, D_MODEL), layer),
        ],
        out_specs=tok(D_MODEL),
        compiler_params=_params("parallel"),
        name="merge",
    )(x, mod, a, y_b, y_c, gt, pool_w, pool_scale, w_pa, w_pb, w_pc, w_out, ln_g, ln_b)


def _rope_tables():
    t = jnp.arange(DEC_SEQ)
    row = (t // GRID_W).astype(F32)
    col = (t % GRID_W).astype(F32)
    nf = HEAD_DIM // 4
    inv = ROPE_BASE ** (-jnp.arange(nf, dtype=F32) / nf)
    ang_row = row[:, None] * inv
    ang_col = col[:, None] * inv

    def half(ang):
        return (jnp.concatenate([jnp.cos(ang), jnp.cos(ang)], axis=-1),
                jnp.concatenate([-jnp.sin(ang), jnp.sin(ang)], axis=-1))

    cr, sr = half(ang_row)
    cc, sc = half(ang_col)
    cos = jnp.concatenate([cr, cc, cr, cc], axis=-1)
    sin = jnp.concatenate([sr, sc, sr, sc], axis=-1)
    return cos, sin


def kernel(x_prompt, x_sample, cache_diff_k, cache_diff_v, cache_na_k, cache_na_v, c, c_ctx, w_mod, b_mod, ln_g, ln_b, ffn1_w1, ffn1_w3, ffn1_w2, ffn2_w1, ffn2_w3, ffn2_w2, w_in, pool_w, pool_scale, w_pa, w_pb, w_pc, lam_q1, lam_k1, lam_q2, lam_k2, subln_g, na_rpb, w_out):
    cond = jnp.concatenate(
        [c_ctx[None, :], c, jnp.zeros((MOD_ROWS - 1 - DEC_BATCH, D_MODEL), F32)], axis=0)
    mod = _modulation(cond, w_mod, b_mod)
    mod = mod[:, :1 + DEC_BATCH].reshape(DEPTH, 1 + DEC_BATCH, N_MOD, D_MODEL)
    mod_ctx, mod_lat = mod[:, :1], mod[:, 1:]

    cache_dkt = cache_diff_k.transpose(0, 1, 3, 4, 5, 2).reshape(DEC_BATCH, DEPTH, DIFF_W, PAST_LEN)
    cache_dv = cache_diff_v.reshape(DEC_BATCH, DEPTH, PAST_LEN, DIFF_W)
    cache_nkt = cache_na_k.transpose(0, 1, 3, 4, 2).reshape(DEC_BATCH, DEPTH, NA_W, PAST_LEN)
    cache_nvt = cache_na_v.transpose(0, 1, 3, 4, 2).reshape(DEC_BATCH, DEPTH, NA_W, PAST_LEN)
    rope_tabs = _rope_tables()

    ffn1 = (ffn1_w1.astype(BF16), ffn1_w3.astype(BF16), ffn1_w2.astype(BF16))
    ffn2 = (ffn2_w1.astype(BF16), ffn2_w3.astype(BF16), ffn2_w2.astype(BF16))
    w_in_b = w_in.astype(BF16)
    w_in_t = jnp.concatenate([w_in[:, :, OFF_KB:OFF_KB + DIFF_W], w_in[:, :, OFF_KC:OFF_KC + 2 * NA_W]],
                             axis=-1).transpose(0, 2, 1).astype(BF16)
    merge_w = (pool_w.astype(BF16), pool_scale.reshape(DEPTH, 1, POOL_WIDTH), w_pa.astype(BF16),
               w_pb.astype(BF16), w_pc.astype(BF16), w_out.astype(BF16))
    lam_args = _lam_args(lam_q1, lam_k1, lam_q2, lam_k2, subln_g)

    x_ctx = x_prompt.reshape(BATCH * SEQ, D_MODEL)
    x_lat = x_sample.reshape(DEC_BATCH * DEC_SEQ, D_MODEL)
    ctx_tiles = (BATCH * SEQ) // TOK_TILE
    lat_tiles = DEC_SEQ // TOK_TILE
    kv_ctx = None

    for l in range(DEPTH):
        lam_init = 0.8 - 0.6 * math.exp(-0.3 * l)

        x_ctx = _ffn(x_ctx, mod_ctx, *ffn1, ln_g, ln_b, layer=l, mod_idx=0, ln_idx=0, tiles_per_group=ctx_tiles)
        a, qb, kbt, vb, qc, kct, vct, gt = _inproj_ctx(x_ctx, mod_ctx, w_in_b, w_in_t, kv_ctx, layer=l)
        kv_ctx = (kbt, vb, kct, vct)
        y_b, y_c = _ctx_attention(qb, kbt, vb, qc, kct, vct, lam_args, layer=l, lam_init=lam_init)
        x_ctx = _merge(x_ctx, mod_ctx, a, y_b, y_c, gt, *merge_w, ln_g, ln_b, layer=l,
                       seg_len=SEQ, tiles_per_group=(BATCH * SEQ) // MIX_TILE)
        x_ctx = _ffn(x_ctx, mod_ctx, *ffn2, ln_g, ln_b, layer=l, mod_idx=2, ln_idx=2, tiles_per_group=ctx_tiles)

        x_lat = _ffn(x_lat, mod_lat, *ffn1, ln_g, ln_b, layer=l, mod_idx=0, ln_idx=0, tiles_per_group=lat_tiles)
        a, qb, kb, vb, qc, kc, vc, gt = _inproj_lat(x_lat, mod_lat, w_in_b, rope_tabs, layer=l)
        seq3 = lambda z: z.reshape(DEC_BATCH, DEC_SEQ, z.shape[-1])
        y_b = _lat_diff_attention(seq3(qb), seq3(kb), seq3(vb), cache_dkt, cache_dv, lam_args,
                                  layer=l, lam_init=lam_init)
        y_c = _lat_na_attention(seq3(qc), seq3(kc), seq3(vc), cache_nkt, cache_nvt,
                                _na_bias_table(na_rpb[l]), layer=l)
        x_lat = _merge(x_lat, mod_lat, a, y_b.reshape(-1, DIFF_W), y_c.reshape(-1, NA_W), gt, *merge_w,
                       ln_g, ln_b, layer=l, seg_len=GRID_W, tiles_per_group=DEC_SEQ // MIX_TILE)
        x_lat = _ffn(x_lat, mod_lat, *ffn2, ln_g, ln_b, layer=l, mod_idx=2, ln_idx=2, tiles_per_group=lat_tiles)

    kbt, vb, kct, vct = kv_ctx
    new_diff_k = kbt.reshape(BATCH, DEPTH, DIFF_HEADS, 2, HEAD_DIM, SEQ).transpose(0, 1, 5, 2, 3, 4)
    new_diff_v = vb.reshape(BATCH, DEPTH, SEQ, DIFF_HEADS, 2 * HEAD_DIM)
    new_na_k = kct.reshape(BATCH, DEPTH, NA_HEADS, HEAD_DIM, SEQ).transpose(0, 1, 4, 2, 3)
    new_na_v = vct.reshape(BATCH, DEPTH, NA_HEADS, HEAD_DIM, SEQ).transpose(0, 1, 4, 2, 3)
    return (x_ctx.reshape(BATCH, SEQ, D_MODEL), x_lat.reshape(DEC_BATCH, DEC_SEQ, D_MODEL),
            new_diff_k, new_diff_v, new_na_k, new_na_v)
```

```python
import functools
import math

import jax
import jax.numpy as jnp
from jax import lax
from jax.experimental import pallas as pl
from jax.experimental.pallas import tpu as pltpu

D_MODEL = 1024
BATCH = 32
SEQ = 256
DEPTH = 2
DEC_BATCH = 2
DEC_SEQ = 2048
PAST_LEN = 512

GRID_W = 64
HEAD_DIM = 64
POOL_WIDTH = D_MODEL // 4
POOL_GROUPS = 4
POOL_GROUP_W = POOL_WIDTH // POOL_GROUPS
POOL_WINDOWS = (2, 4, 8, 16)
DIFF_HEADS = (D_MODEL // 2) // (2 * HEAD_DIM)
DIFF_W = DIFF_HEADS * 2 * HEAD_DIM
NA_HEADS = (D_MODEL // 4) // HEAD_DIM
NA_W = NA_HEADS * HEAD_DIM
NA_ROWS = 8
NA_COLS = 16
N_BRANCH = 3
D_IN = POOL_WIDTH + 3 * DIFF_W + 3 * NA_W + N_BRANCH * D_MODEL
D_FF = ((8 * D_MODEL // 3 + 127) // 128) * 128
N_MOD = 9
ROPE_BASE = 10000.0
LN_EPS = 1e-5
RMS_EPS = 1e-5
ALPHA = (2 * DEPTH) ** 0.25
ATTN_SCALE = HEAD_DIM ** -0.5
NEG_INF = -1e30

OFF_A = 0
OFF_QB = OFF_A + POOL_WIDTH
OFF_KB = OFF_QB + DIFF_W
OFF_VB = OFF_KB + DIFF_W
OFF_QC = OFF_VB + DIFF_W
OFF_KC = OFF_QC + NA_W
OFF_VC = OFF_KC + NA_W
OFF_GT = OFF_VC + NA_W
KT_W = DIFF_W + 2 * NA_W

LANES = 128
MXU_DIM = 256
VMEM_LIMIT = 56 * 1024 * 1024

FF_CHUNK = MXU_DIM
N_FF_CHUNKS = D_FF // FF_CHUNK
TOK_TILE = 512
FFN_TILE = 1024
MIX_TILE = 256
MERGE_TILE = 512
POOL_BLOCK = 256
MOD_ROWS = 16
MOD_COLS = 1024
NA_Q_ROWS = 4
NA_K_ROWS = NA_Q_ROWS + NA_ROWS
NA_GRID_ROWS = DEC_SEQ // GRID_W
NA_TILES = NA_GRID_ROWS // NA_Q_ROWS
NA_CASE_TILES = (0, 1, NA_TILES - 1)
SEQ_PER_TILE = TOK_TILE // SEQ

BF16 = jnp.bfloat16
F32 = jnp.float32


def _dot(a, b):
    return jnp.dot(a, b, preferred_element_type=F32)


def _dot_nt(a, b):
    return lax.dot_general(a, b, (((1,), (1,)), ((), ())), preferred_element_type=F32)


def _sigmoid(x):
    return 1.0 / (1.0 + jnp.exp(-x))


def _layer_norm(y, g, b):
    mu = jnp.mean(y, axis=-1, keepdims=True)
    d = y - mu
    var = jnp.mean(d * d, axis=-1, keepdims=True)
    return d * lax.rsqrt(var + LN_EPS) * g + b


def _split3(x):
    hi = x.astype(BF16)
    r = x - hi.astype(F32)
    mid = r.astype(BF16)
    lo = (r - mid.astype(F32)).astype(BF16)
    return hi, mid, lo


def _resident(shape, index=None):
    index = (0,) * len(shape) if index is None else index
    return pl.BlockSpec(shape, lambda *_: index, pipeline_mode=pl.Buffered(1))


def _layer_block(shape, layer):
    return _resident((1,) + shape, (layer,) + (0,) * len(shape))


def _params(*sem):
    return pltpu.CompilerParams(dimension_semantics=sem, vmem_limit_bytes=VMEM_LIMIT)


def _mod_kernel(cond_ref, w_ref, b_ref, o_ref):
    c = cond_ref[...]
    x = c * _sigmoid(c)
    w = w_ref[0]
    x_hi = x.astype(BF16)
    x_lo = (x - x_hi.astype(F32)).astype(BF16)
    w_hi = w.astype(BF16)
    w_lo = (w - w_hi.astype(F32)).astype(BF16)
    o_ref[0] = _dot(x_hi, w_hi) + _dot(x_lo, w_hi) + _dot(x_hi, w_lo) + b_ref[0]


def _modulation(cond, w_mod, b_mod):
    n_cols = N_MOD * D_MODEL
    return pl.pallas_call(
        _mod_kernel,
        out_shape=jax.ShapeDtypeStruct((DEPTH, MOD_ROWS, n_cols), F32),
        grid=(DEPTH, n_cols // MOD_COLS),
        in_specs=[
            pl.BlockSpec((MOD_ROWS, D_MODEL), lambda l, j: (0, 0)),
            pl.BlockSpec((1, D_MODEL, MOD_COLS), lambda l, j: (l, 0, j)),
            pl.BlockSpec((1, 1, MOD_COLS), lambda l, j: (l, 0, j)),
        ],
        out_specs=pl.BlockSpec((1, MOD_ROWS, MOD_COLS), lambda l, j: (l, 0, j)),
        compiler_params=_params("parallel", "parallel"),
        name="modulation",
    )(cond, w_mod, b_mod.reshape(DEPTH, 1, n_cols))


def _ffn_kernel(x_ref, m_ref, w1_ref, w3_ref, w2_ref, g_ref, b_ref, o_ref, xm_ref, acc_ref, *, mod_idx, ln_idx):
    x = x_ref[...]
    m = m_ref[0, 0]
    shift = m[3 * mod_idx:3 * mod_idx + 1]
    scale = m[3 * mod_idx + 1:3 * mod_idx + 2]
    gate = m[3 * mod_idx + 2:3 * mod_idx + 3]
    xm_ref[...] = (x * (1.0 + scale) + shift).astype(BF16)

    def chunk_out(c):
        cols = pl.ds(pl.multiple_of(c * FF_CHUNK, FF_CHUNK), FF_CHUNK)
        xm = xm_ref[...]
        h1 = _dot(xm, w1_ref[0, :, cols])
        h3 = _dot(xm, w3_ref[0, :, cols])
        act = (h1 * _sigmoid(h1)) * h3
        return _dot(act.astype(BF16), w2_ref[0, cols, :])

    acc_ref[...] = chunk_out(0)

    def chunk(c, carry):
        acc_ref[...] += chunk_out(c)
        return carry

    lax.fori_loop(1, N_FF_CHUNKS, chunk, 0)
    y = ALPHA * x + (0.5 * gate) * acc_ref[...]
    o_ref[...] = _layer_norm(y, g_ref[0, ln_idx:ln_idx + 1], b_ref[0, ln_idx:ln_idx + 1])


def _mod_spec(layer, tiles_per_group):
    return pl.BlockSpec((1, 1, N_MOD, D_MODEL), lambda i: (layer, i // tiles_per_group, 0, 0))


def _ffn(x, mod, w1, w3, w2, ln_g, ln_b, *, layer, mod_idx, ln_idx, tiles_per_group):
    n_tok = x.shape[0]
    return pl.pallas_call(
        functools.partial(_ffn_kernel, mod_idx=mod_idx, ln_idx=ln_idx),
        out_shape=jax.ShapeDtypeStruct((n_tok, D_MODEL), F32),
        grid=(n_tok // FFN_TILE,),
        in_specs=[
            pl.BlockSpec((FFN_TILE, D_MODEL), lambda i: (i, 0)),
            _mod_spec(layer, tiles_per_group),
            _layer_block((D_MODEL, D_FF), layer),
            _layer_block((D_MODEL, D_FF), layer),
            _layer_block((D_FF, D_MODEL), layer),
            _layer_block((3, D_MODEL), layer),
            _layer_block((3, D_MODEL), layer),
        ],
        out_specs=pl.BlockSpec((FFN_TILE, D_MODEL), lambda i: (i, 0)),
        scratch_shapes=[pltpu.VMEM((FFN_TILE, D_MODEL), BF16), pltpu.VMEM((FFN_TILE, D_MODEL), F32)],
        compiler_params=_params("parallel"),
        name="ffn",
    )(x, mod, w1, w3, w2, ln_g, ln_b)


def _rope(x, cos, sin_signed):
    lane = lax.broadcasted_iota(jnp.int32, x.shape, 1)
    first = (lane % (HEAD_DIM // 2)) < (HEAD_DIM // 4)
    partner = jnp.where(first, pltpu.roll(x, LANES - HEAD_DIM // 4, 1), pltpu.roll(x, HEAD_DIM // 4, 1))
    return x * cos + partner * sin_signed


def _modulated(x_ref, m_ref):
    m = m_ref[0, 0]
    return (x_ref[...] * (1.0 + m[4:5]) + m[3:4]).astype(BF16)


def _inproj_ctx_kernel(*refs):
    x_ref, m_ref, w_ref, wt_ref = refs[:4]
    a_ref, qb_ref, kbt_ref, vb_ref, qc_ref, kct_ref, vct_ref, gt_ref = refs[-8:]
    xm = _modulated(x_ref, m_ref)

    def proj(off, width):
        return _dot(xm, w_ref[0, :, off:off + width])

    a_ref[...] = proj(OFF_A, POOL_WIDTH)
    qb_ref[...] = (proj(OFF_QB, DIFF_W) * ATTN_SCALE).astype(qb_ref.dtype)
    qc_ref[...] = (proj(OFF_QC, NA_W) * ATTN_SCALE).astype(qc_ref.dtype)
    v = proj(OFF_VB, DIFF_W)
    for j in range(N_BRANCH):
        gt_ref[:, j * D_MODEL:(j + 1) * D_MODEL] = proj(OFF_GT + j * D_MODEL, D_MODEL)
    zt = _dot_nt(wt_ref[0], xm)
    for s in range(SEQ_PER_TILE):
        tok = slice(s * SEQ, (s + 1) * SEQ)
        vb_ref[s, 0] = v[tok]
        kbt_ref[s, 0] = zt[:DIFF_W, tok]
        kct_ref[s, 0] = zt[DIFF_W:DIFF_W + NA_W, tok]
        vct_ref[s, 0] = zt[DIFF_W + NA_W:, tok]


def _inproj_ctx(x, mod, w_in, w_in_t, kv_prev, *, layer):
    n_tok = x.shape[0]
    tok = lambda width: pl.BlockSpec((TOK_TILE, width), lambda i: (i, 0))
    kv = lambda rows, cols: pl.BlockSpec((SEQ_PER_TILE, 1, rows, cols), lambda i: (i, layer, 0, 0))
    kv_shape = lambda rows, cols: jax.ShapeDtypeStruct((BATCH, DEPTH, rows, cols), F32)
    in_specs = [
        tok(D_MODEL),
        _mod_spec(layer, n_tok // TOK_TILE),
        _layer_block((D_MODEL, D_IN), layer),
        _layer_block((KT_W, D_MODEL), layer),
    ]
    args = [x, mod, w_in, w_in_t]
    aliases = {}
    if kv_prev is not None:
        for arr, out_idx in zip(kv_prev, (2, 3, 5, 6)):
            aliases[len(args)] = out_idx
            args.append(arr)
            in_specs.append(pl.BlockSpec(memory_space=pl.ANY))
    return pl.pallas_call(
        _inproj_ctx_kernel,
        out_shape=[
            jax.ShapeDtypeStruct((n_tok, POOL_WIDTH), F32),
            jax.ShapeDtypeStruct((n_tok, DIFF_W), BF16),
            kv_shape(DIFF_W, SEQ),
            kv_shape(SEQ, DIFF_W),
            jax.ShapeDtypeStruct((n_tok, NA_W), BF16),
            kv_shape(NA_W, SEQ),
            kv_shape(NA_W, SEQ),
            jax.ShapeDtypeStruct((n_tok, N_BRANCH * D_MODEL), F32),
        ],
        grid=(n_tok // TOK_TILE,),
        in_specs=in_specs,
        out_specs=[tok(POOL_WIDTH), tok(DIFF_W), kv(DIFF_W, SEQ), kv(SEQ, DIFF_W),
                   tok(NA_W), kv(NA_W, SEQ), kv(NA_W, SEQ), tok(N_BRANCH * D_MODEL)],
        input_output_aliases=aliases,
        compiler_params=_params("parallel"),
        name="inproj_ctx",
    )(*args)


def _inproj_lat_kernel(x_ref, m_ref, w_ref, cos_ref, sin_ref,
                       a_ref, qb_ref, kb_ref, vb_ref, qc_ref, kc_ref, vc_ref, gt_ref):
    xm = _modulated(x_ref, m_ref)

    def proj(off, width):
        return _dot(xm, w_ref[0, :, off:off + width])

    a_ref[...] = proj(OFF_A, POOL_WIDTH)
    for h in range(DIFF_HEADS):
        lo = h * 2 * HEAD_DIM
        q = _rope(proj(OFF_QB + lo, 2 * HEAD_DIM), cos_ref[...], sin_ref[...])
        k = _rope(proj(OFF_KB + lo, 2 * HEAD_DIM), cos_ref[...], sin_ref[...])
        qb_ref[:, lo:lo + 2 * HEAD_DIM] = (q * ATTN_SCALE).astype(qb_ref.dtype)
        kb_ref[:, lo:lo + 2 * HEAD_DIM] = k.astype(kb_ref.dtype)
    vb_ref[...] = proj(OFF_VB, DIFF_W).astype(vb_ref.dtype)
    qc_ref[...] = (proj(OFF_QC, NA_W) * ATTN_SCALE).astype(qc_ref.dtype)
    kc_ref[...] = proj(OFF_KC, NA_W).astype(kc_ref.dtype)
    vc_ref[...] = proj(OFF_VC, NA_W).astype(vc_ref.dtype)
    for j in range(N_BRANCH):
        gt_ref[:, j * D_MODEL:(j + 1) * D_MODEL] = proj(OFF_GT + j * D_MODEL, D_MODEL)


def _inproj_lat(x, mod, w_in, rope_tabs, *, layer):
    n_tok = x.shape[0]
    tok = lambda width: pl.BlockSpec((TOK_TILE, width), lambda i: (i, 0))
    tiles_per_seq = DEC_SEQ // TOK_TILE
    widths = (POOL_WIDTH, DIFF_W, DIFF_W, DIFF_W, NA_W, NA_W, NA_W, N_BRANCH * D_MODEL)
    dtypes = (F32, BF16, BF16, BF16, BF16, BF16, BF16, F32)
    return pl.pallas_call(
        _inproj_lat_kernel,
        out_shape=[jax.ShapeDtypeStruct((n_tok, w), d) for w, d in zip(widths, dtypes)],
        grid=(n_tok // TOK_TILE,),
        in_specs=[
            tok(D_MODEL),
            _mod_spec(layer, tiles_per_seq),
            _layer_block((D_MODEL, D_IN), layer),
            pl.BlockSpec((TOK_TILE, LANES), lambda i: (i % tiles_per_seq, 0)),
            pl.BlockSpec((TOK_TILE, LANES), lambda i: (i % tiles_per_seq, 0)),
        ],
        out_specs=[tok(w) for w in widths],
        compiler_params=_params("parallel"),
        name="inproj_lat",
    )(x, mod, w_in, *rope_tabs)


def _diff_lambda(lq1, lk1, lq2, lk2, lam_init):
    e1 = jnp.exp(jnp.sum(lq1 * lk1, axis=-1, keepdims=True))
    e2 = jnp.exp(jnp.sum(lq2 * lk2, axis=-1, keepdims=True))
    return e1 - e2 + lam_init


def _map_masks(q):
    lane = lax.broadcasted_iota(jnp.int32, q.shape, 1)
    zero = jnp.zeros_like(q)
    return jnp.where(lane < HEAD_DIM, q, zero), jnp.where(lane >= HEAD_DIM, q, zero)


def _head_mask(shape, h):
    lane = lax.broadcasted_iota(jnp.int32, shape, 1)
    return (lane >= h * HEAD_DIM) & (lane < (h + 1) * HEAD_DIM)


def _sub_ln(o, g, lam_init):
    return o * lax.rsqrt(jnp.mean(o * o, axis=-1, keepdims=True) + RMS_EPS) * g * (1.0 - lam_init)


_LAM_SPECS = lambda layer: [_layer_block((1, HEAD_DIM), layer)] * 4 + [_layer_block((1, 2 * HEAD_DIM), layer)]


def _lam_args(lam_q1, lam_k1, lam_q2, lam_k2, subln_g):
    vec = lambda a: a.reshape(DEPTH, 1, a.shape[-1])
    return vec(lam_q1), vec(lam_k1), vec(lam_q2), vec(lam_k2), vec(subln_g)


def _ctx_attn_kernel(qb_ref, kbt_ref, vb_ref, qc_ref, kct_ref, vct_ref,
                     lq1_ref, lk1_ref, lq2_ref, lk2_ref, g_ref, yb_ref, yc_ref, *, lam_init):
    lam = _diff_lambda(lq1_ref[0], lk1_ref[0], lq2_ref[0], lk2_ref[0], lam_init)
    for h in range(DIFF_HEADS):
        sl = slice(h * 2 * HEAD_DIM, (h + 1) * 2 * HEAD_DIM)
        q0, q1 = _map_masks(qb_ref[:, sl])
        kt = kbt_ref[0, 0, sl, :].astype(BF16)
        v = vb_ref[0, 0, :, sl].astype(BF16)
        probs = []
        for q in (q0, q1):
            s = _dot(q, kt)
            e = jnp.exp(s - jnp.max(s, axis=-1, keepdims=True))
            probs.append(e * (1.0 / jnp.sum(e, axis=-1, keepdims=True)))
        a = (probs[0] - lam * probs[1]).astype(BF16)
        yb_ref[:, sl] = _sub_ln(_dot(a, v), g_ref[0], lam_init).astype(yb_ref.dtype)

    qc = qc_ref[...]
    kct = kct_ref[0, 0].astype(BF16)
    vct = vct_ref[0, 0].astype(BF16)
    out = jnp.zeros(qc.shape, F32)
    for h in range(NA_HEADS):
        mask = _head_mask(qc.shape, h)
        s = _dot(jnp.where(mask, qc, jnp.zeros_like(qc)), kct)
        e = jnp.exp(s - jnp.max(s, axis=-1, keepdims=True))
        p = (e * (1.0 / jnp.sum(e, axis=-1, keepdims=True))).astype(BF16)
        out = out + jnp.where(mask, _dot_nt(p, vct), 0.0)
    yc_ref[...] = out.astype(yc_ref.dtype)


def _ctx_attention(qb, kbt, vb, qc, kct, vct, lam_args, *, layer, lam_init):
    n_tok = qb.shape[0]
    seq = lambda width: pl.BlockSpec((SEQ, width), lambda b: (b, 0))
    kv = lambda rows, cols: pl.BlockSpec((1, 1, rows, cols), lambda b: (b, layer, 0, 0))
    return pl.pallas_call(
        functools.partial(_ctx_attn_kernel, lam_init=lam_init),
        out_shape=[jax.ShapeDtypeStruct((n_tok, DIFF_W), BF16), jax.ShapeDtypeStruct((n_tok, NA_W), BF16)],
        grid=(n_tok // SEQ,),
        in_specs=[seq(DIFF_W), kv(DIFF_W, SEQ), kv(SEQ, DIFF_W), seq(NA_W), kv(NA_W, SEQ), kv(NA_W, SEQ)]
                 + _LAM_SPECS(layer),
        out_specs=[seq(DIFF_W), seq(NA_W)],
        compiler_params=_params("parallel"),
        name="ctx_attention",
    )(qb, kbt, vb, qc, kct, vct, *lam_args)


def _lat_diff_kernel(q_ref, k_ref, v_ref, ckt_ref, cv_ref,
                     lq1_ref, lk1_ref, lq2_ref, lk2_ref, g_ref, y_ref, *, lam_init):
    lam = _diff_lambda(lq1_ref[0], lk1_ref[0], lq2_ref[0], lk2_ref[0], lam_init)
    for h in range(DIFF_HEADS):
        sl = slice(h * 2 * HEAD_DIM, (h + 1) * 2 * HEAD_DIM)
        q0, q1 = _map_masks(q_ref[0, :, sl])
        k = k_ref[0, :, sl]
        v = v_ref[0, :, sl]
        ckt = ckt_ref[0, 0, sl, :].astype(BF16)
        cv = cv_ref[0, 0, :, sl].astype(BF16)
        parts = []
        for q in (q0, q1):
            s_new = _dot_nt(q, k)
            s_old = _dot(q, ckt)
            top = jnp.maximum(jnp.max(s_new, axis=-1, keepdims=True), jnp.max(s_old, axis=-1, keepdims=True))
            e_new = jnp.exp(s_new - top)
            e_old = jnp.exp(s_old - top)
            inv = 1.0 / (jnp.sum(e_new, axis=-1, keepdims=True) + jnp.sum(e_old, axis=-1, keepdims=True))
            parts.append((e_new, e_old, inv))
        (n0, o0, r0), (n1, o1, r1) = parts
        r1 = lam * r1
        a_new = (n0 * r0 - n1 * r1).astype(BF16)
        a_old = (o0 * r0 - o1 * r1).astype(BF16)
        o = _dot(a_new, v) + _dot(a_old, cv)
        y_ref[0, :, sl] = _sub_ln(o, g_ref[0], lam_init).astype(y_ref.dtype)


def _lat_diff_attention(qb, kb, vb, cache_kt, cache_v, lam_args, *, layer, lam_init):
    full = pl.BlockSpec((1, DEC_SEQ, DIFF_W), lambda b, i: (b, 0, 0))
    past = lambda rows, cols: pl.BlockSpec((1, 1, rows, cols), lambda b, i: (b, layer, 0, 0))
    tile = pl.BlockSpec((1, MIX_TILE, DIFF_W), lambda b, i: (b, i, 0))
    return pl.pallas_call(
        functools.partial(_lat_diff_kernel, lam_init=lam_init),
        out_shape=jax.ShapeDtypeStruct((DEC_BATCH, DEC_SEQ, DIFF_W), BF16),
        grid=(DEC_BATCH, DEC_SEQ // MIX_TILE),
        in_specs=[tile, full, full, past(DIFF_W, PAST_LEN), past(PAST_LEN, DIFF_W)] + _LAM_SPECS(layer),
        out_specs=tile,
        compiler_params=_params("parallel", "parallel"),
        name="latent_diff_attention",
    )(qb, kb, vb, cache_kt, cache_v, *lam_args)


def _na_window_start(tile):
    return jnp.clip(tile * NA_Q_ROWS - NA_ROWS // 2, 0, NA_GRID_ROWS - NA_K_ROWS)


def _toeplitz(v):
    n = GRID_W
    period = 2 * n - 1
    reps = [1] * (v.ndim - 1) + [n + 1]
    flat = jnp.tile(v, reps)[..., :n * 2 * n]
    skew = flat.reshape(*v.shape[:-1], n, 2 * n)
    return skew[..., ::-1, :n]


def _na_bias_table(rpb):
    edge = GRID_W - NA_COLS
    ext = jnp.concatenate([jnp.repeat(rpb[..., :1], edge, axis=-1), rpb,
                           jnp.repeat(rpb[..., -1:], edge, axis=-1)], axis=-1)
    blocks = _toeplitz(ext)
    col = jnp.arange(GRID_W)
    col_start = jnp.clip(col - NA_COLS // 2, 0, GRID_W - NA_COLS)
    col_ok = (col[None, :] >= col_start[:, None]) & (col[None, :] < col_start[:, None] + NA_COLS)
    blocks = jnp.where(col_ok, blocks, NEG_INF)
    masked = jnp.full((NA_HEADS, GRID_W, GRID_W), NEG_INF, F32)
    tables = []
    for tile in NA_CASE_TILES:
        k0 = min(max(tile * NA_Q_ROWS - NA_ROWS // 2, 0), NA_GRID_ROWS - NA_K_ROWS)
        q_rows = []
        for qr in range(tile * NA_Q_ROWS, (tile + 1) * NA_Q_ROWS):
            row_start = min(max(qr - NA_ROWS // 2, 0), NA_GRID_ROWS - NA_ROWS)
            parts = []
            for kr in range(k0, k0 + NA_K_ROWS):
                inside = row_start <= kr < row_start + NA_ROWS
                parts.append(blocks[:, kr - qr + NA_ROWS - 1] if inside else masked)
            q_rows.append(jnp.concatenate(parts, axis=-1))
        tables.append(jnp.concatenate(q_rows, axis=1))
    return jnp.stack(tables).astype(F32)


def _lat_na_kernel(q_ref, k_ref, v_ref, ckt_ref, cvt_ref, bias_ref, y_ref):
    start = _na_window_start(pl.program_id(1))
    win = pl.ds(pl.multiple_of(start * GRID_W, NA_Q_ROWS * GRID_W), NA_K_ROWS * GRID_W)
    k = k_ref[0, win, :]
    v = v_ref[0, win, :]
    ckt = ckt_ref[0, 0].astype(BF16)
    cvt = cvt_ref[0, 0].astype(BF16)
    q = q_ref[0]
    out = jnp.zeros(q.shape, F32)
    for h in range(NA_HEADS):
        mask = _head_mask(q.shape, h)
        qh = jnp.where(mask, q, jnp.zeros_like(q))
        s_nb = _dot_nt(qh, k) + bias_ref[0, h]
        s_ctx = _dot(qh, ckt)
        top = jnp.maximum(jnp.max(s_nb, axis=-1, keepdims=True), jnp.max(s_ctx, axis=-1, keepdims=True))
        e_nb = jnp.exp(s_nb - top)
        e_ctx = jnp.exp(s_ctx - top)
        inv = 1.0 / (jnp.sum(e_nb, axis=-1, keepdims=True) + jnp.sum(e_ctx, axis=-1, keepdims=True))
        o = _dot((e_nb * inv).astype(BF16), v) + _dot_nt((e_ctx * inv).astype(BF16), cvt)
        out = out + jnp.where(mask, o, 0.0)
    y_ref[0] = out.astype(y_ref.dtype)


def _lat_na_attention(qc, kc, vc, cache_kt, cache_vt, bias, *, layer):
    q_tok = NA_Q_ROWS * GRID_W

    def bias_index(b, i):
        return (jnp.where(i == 0, 0, jnp.where(i == NA_TILES - 1, 2, 1)), 0, 0, 0)

    full = pl.BlockSpec((1, DEC_SEQ, NA_W), lambda b, i: (b, 0, 0))
    past = pl.BlockSpec((1, 1, NA_W, PAST_LEN), lambda b, i: (b, layer, 0, 0))
    tile = pl.BlockSpec((1, q_tok, NA_W), lambda b, i: (b, i, 0))
    return pl.pallas_call(
        _lat_na_kernel,
        out_shape=jax.ShapeDtypeStruct((DEC_BATCH, DEC_SEQ, NA_W), BF16),
        grid=(DEC_BATCH, NA_TILES),
        in_specs=[tile, full, full, past, past,
                  pl.BlockSpec((1, NA_HEADS, q_tok, NA_K_ROWS * GRID_W), bias_index)],
        out_specs=tile,
        compiler_params=_params("parallel", "parallel"),
        name="latent_na_attention",
    )(qc, kc, vc, cache_kt, cache_vt, bias)


def _pool(a_ref, pool_w_ref, pool_scale, seg_len):
    n = POOL_BLOCK
    t = lax.broadcasted_iota(jnp.int32, (n, n), 0)
    j = lax.broadcasted_iota(jnp.int32, (n, n), 1)
    shift = seg_len.bit_length() - 1
    same_seg = jnp.right_shift(t, shift) == jnp.right_shift(j, shift)
    bands, counts = [], []
    for w in POOL_WINDOWS:
        band = jnp.where(same_seg & (j >= t - w // 2) & (j < t + w // 2), 1.0, 0.0)
        counts.append(jnp.sum(band, axis=-1, keepdims=True))
        bands.append(band.astype(BF16))
    blocks = []
    for r in range(MERGE_TILE // POOL_BLOCK):
        outs = []
        for g in range(POOL_GROUPS):
            a_g = a_ref[r * POOL_BLOCK:(r + 1) * POOL_BLOCK, g * POOL_GROUP_W:(g + 1) * POOL_GROUP_W]
            hi, mid, lo = _split3(a_g)
            total = _dot(bands[g], hi) + _dot(bands[g], mid) + _dot(bands[g], lo)
            p = total / counts[g] - a_g
            outs.append(_dot(p.astype(BF16), pool_w_ref[0, g]))
        blocks.append(jnp.concatenate(outs, axis=-1) * pool_scale)
    return jnp.concatenate(blocks, axis=0)


def _merge_kernel(x_ref, m_ref, a_ref, yb_ref, yc_ref, gt_ref, pool_w_ref, pool_scale_ref,
                  wpa_ref, wpb_ref, wpc_ref, wout_ref, g_ref, b_ref, o_ref, *, seg_len):
    y_a = _pool(a_ref, pool_w_ref, pool_scale_ref[0], seg_len)
    branches = (
        _dot(y_a.astype(BF16), wpa_ref[0]),
        _dot(yb_ref[...], wpb_ref[0]),
        _dot(yc_ref[...], wpc_ref[0]),
    )
    mixed = None
    for j, br in enumerate(branches):
        term = _sigmoid(gt_ref[:, j * D_MODEL:(j + 1) * D_MODEL]) * br
        mixed = term if mixed is None else mixed + term
    h = _dot(mixed.astype(BF16), wout_ref[0])
    m = m_ref[0, 0]
    y = ALPHA * x_ref[...] + m[5:6] * h
    o_ref[...] = _layer_norm(y, g_ref[0, 1:2], b_ref[0, 1:2])


def _merge(x, mod, a, y_b, y_c, gt, pool_w, pool_scale, w_pa, w_pb, w_pc, w_out, ln_g, ln_b,
           *, layer, seg_len, tiles_per_group):
    n_tok = x.shape[0]
    tok = lambda width: pl.BlockSpec((MERGE_TILE, width), lambda i: (i, 0))
    return pl.pallas_call(
        functools.partial(_merge_kernel, seg_len=seg_len),
        out_shape=jax.ShapeDtypeStruct((n_tok, D_MODEL), F32),
        grid=(n_tok // MERGE_TILE,),
        in_specs=[
            tok(D_MODEL),
            _mod_spec(layer, tiles_per_group),
            tok(POOL_WIDTH), tok(DIFF_W), tok(NA_W), tok(N_BRANCH * D_MODEL),
            _layer_block((POOL_GROUPS, POOL_GROUP_W, POOL_GROUP_W), layer),
            _layer_block((1, POOL_WIDTH), layer),
            _layer_block((POOL_WIDTH, D_MODEL), layer),
            _layer_block((DIFF_W, D_MODEL), layer),
            _layer_block((NA_W, D_MODEL), layer),
            _layer_block((D_MODEL, D_MODEL), layer),
            _layer_block((3, D_MODEL), layer),
            _layer_block((3, D_MODEL), layer),
        ],
        out_specs=tok(D_MODEL),
        compiler_params=_params("parallel"),
        name="merge",
    )(x, mod, a, y_b, y_c, gt, pool_w, pool_scale, w_pa, w_pb, w_pc, w_out, ln_g, ln_b)


def _rope_tables():
    t = jnp.arange(DEC_SEQ)
    row = (t // GRID_W).astype(F32)
    col = (t % GRID_W).astype(F32)
    nf = HEAD_DIM // 4
    inv = ROPE_BASE ** (-jnp.arange(nf, dtype=F32) / nf)
    ang_row = row[:, None] * inv
    ang_col = col[:, None] * inv

    def half(ang):
        return (jnp.concatenate([jnp.cos(ang), jnp.cos(ang)], axis=-1),
                jnp.concatenate([-jnp.sin(ang), jnp.sin(ang)], axis=-1))

    cr, sr = half(ang_row)
    cc, sc = half(ang_col)
    cos = jnp.concatenate([cr, cc, cr, cc], axis=-1)
    sin = jnp.concatenate([sr, sc, sr, sc], axis=-1)
    return cos, sin


def kernel(x_prompt, x_sample, cache_diff_k, cache_diff_v, cache_na_k, cache_na_v, c, c_ctx, w_mod, b_mod, ln_g, ln_b, ffn1_w1, ffn1_w3, ffn1_w2, ffn2_w1, ffn2_w3, ffn2_w2, w_in, pool_w, pool_scale, w_pa, w_pb, w_pc, lam_q1, lam_k1, lam_q2, lam_k2, subln_g, na_rpb, w_out):
    cond = jnp.concatenate(
        [c_ctx[None, :], c, jnp.zeros((MOD_ROWS - 1 - DEC_BATCH, D_MODEL), F32)], axis=0)
    mod = _modulation(cond, w_mod, b_mod)
    mod = mod[:, :1 + DEC_BATCH].reshape(DEPTH, 1 + DEC_BATCH, N_MOD, D_MODEL)
    mod_ctx, mod_lat = mod[:, :1], mod[:, 1:]

    cache_dkt = cache_diff_k.transpose(0, 1, 3, 4, 5, 2).reshape(DEC_BATCH, DEPTH, DIFF_W, PAST_LEN)
    cache_dv = cache_diff_v.reshape(DEC_BATCH, DEPTH, PAST_LEN, DIFF_W)
    cache_nkt = cache_na_k.transpose(0, 1, 3, 4, 2).reshape(DEC_BATCH, DEPTH, NA_W, PAST_LEN)
    cache_nvt = cache_na_v.transpose(0, 1, 3, 4, 2).reshape(DEC_BATCH, DEPTH, NA_W, PAST_LEN)
    rope_tabs = _rope_tables()

    ffn1 = (ffn1_w1.astype(BF16), ffn1_w3.astype(BF16), ffn1_w2.astype(BF16))
    ffn2 = (ffn2_w1.astype(BF16), ffn2_w3.astype(BF16), ffn2_w2.astype(BF16))
    w_in_b = w_in.astype(BF16)
    w_in_t = jnp.concatenate([w_in[:, :, OFF_KB:OFF_KB + DIFF_W], w_in[:, :, OFF_KC:OFF_KC + 2 * NA_W]],
                             axis=-1).transpose(0, 2, 1).astype(BF16)
    merge_w = (pool_w.astype(BF16), pool_scale.reshape(DEPTH, 1, POOL_WIDTH), w_pa.astype(BF16),
               w_pb.astype(BF16), w_pc.astype(BF16), w_out.astype(BF16))
    lam_args = _lam_args(lam_q1, lam_k1, lam_q2, lam_k2, subln_g)

    x_ctx = x_prompt.reshape(BATCH * SEQ, D_MODEL)
    x_lat = x_sample.reshape(DEC_BATCH * DEC_SEQ, D_MODEL)
    ctx_tiles = (BATCH * SEQ) // FFN_TILE
    lat_tiles = DEC_SEQ // FFN_TILE
    kv_ctx = None

    for l in range(DEPTH):
        lam_init = 0.8 - 0.6 * math.exp(-0.3 * l)

        x_ctx = _ffn(x_ctx, mod_ctx, *ffn1, ln_g, ln_b, layer=l, mod_idx=0, ln_idx=0, tiles_per_group=ctx_tiles)
        a, qb, kbt, vb, qc, kct, vct, gt = _inproj_ctx(x_ctx, mod_ctx, w_in_b, w_in_t, kv_ctx, layer=l)
        kv_ctx = (kbt, vb, kct, vct)
        y_b, y_c = _ctx_attention(qb, kbt, vb, qc, kct, vct, lam_args, layer=l, lam_init=lam_init)
        x_ctx = _merge(x_ctx, mod_ctx, a, y_b, y_c, gt, *merge_w, ln_g, ln_b, layer=l,
                       seg_len=SEQ, tiles_per_group=(BATCH * SEQ) // MERGE_TILE)
        x_ctx = _ffn(x_ctx, mod_ctx, *ffn2, ln_g, ln_b, layer=l, mod_idx=2, ln_idx=2, tiles_per_group=ctx_tiles)

        x_lat = _ffn(x_lat, mod_lat, *ffn1, ln_g, ln_b, layer=l, mod_idx=0, ln_idx=0, tiles_per_group=lat_tiles)
        a, qb, kb, vb, qc, kc, vc, gt = _inproj_lat(x_lat, mod_lat, w_in_b, rope_tabs, layer=l)
        seq3 = lambda z: z.reshape(DEC_BATCH, DEC_SEQ, z.shape[-1])
        y_b = _lat_diff_attention(seq3(qb), seq3(kb), seq3(vb), cache_dkt, cache_dv, lam_args,
                                  layer=l, lam_init=lam_init)
        y_c = _lat_na_attention(seq3(qc), seq3(kc), seq3(vc), cache_nkt, cache_nvt,
                                _na_bias_table(na_rpb[l]), layer=l)
        x_lat = _merge(x_lat, mod_lat, a, y_b.reshape(-1, DIFF_W), y_c.reshape(-1, NA_W), gt, *merge_w,
                       ln_g, ln_b, layer=l, seg_len=GRID_W, tiles_per_group=DEC_SEQ // MERGE_TILE)
        x_lat = _ffn(x_lat, mod_lat, *ffn2, ln_g, ln_b, layer=l, mod_idx=2, ln_idx=2, tiles_per_group=lat_tiles)

    kbt, vb, kct, vct = kv_ctx
    new_diff_k = kbt.reshape(BATCH, DEPTH, DIFF_HEADS, 2, HEAD_DIM, SEQ).transpose(0, 1, 5, 2, 3, 4)
    new_diff_v = vb.reshape(BATCH, DEPTH, SEQ, DIFF_HEADS, 2 * HEAD_DIM)
    new_na_k = kct.reshape(BATCH, DEPTH, NA_HEADS, HEAD_DIM, SEQ).transpose(0, 1, 4, 2, 3)
    new_na_v = vct.reshape(BATCH, DEPTH, NA_HEADS, HEAD_DIM, SEQ).transpose(0, 1, 4, 2, 3)
    return (x_ctx.reshape(BATCH, SEQ, D_MODEL), x_lat.reshape(DEC_BATCH, DEC_SEQ, D_MODEL),
            new_diff_k, new_diff_v, new_na_k, new_na_v)
```

```python
import functools
import math

import jax
import jax.numpy as jnp
from jax import lax
from jax.experimental import pallas as pl
from jax.experimental.pallas import tpu as pltpu

D_MODEL = 1024
BATCH = 32
SEQ = 256
DEPTH = 2
DEC_BATCH = 2
DEC_SEQ = 2048
PAST_LEN = 512

GRID_W = 64
HEAD_DIM = 64
POOL_WIDTH = D_MODEL // 4
POOL_GROUPS = 4
POOL_GROUP_W = POOL_WIDTH // POOL_GROUPS
POOL_WINDOWS = (2, 4, 8, 16)
DIFF_HEADS = (D_MODEL // 2) // (2 * HEAD_DIM)
DIFF_W = DIFF_HEADS * 2 * HEAD_DIM
NA_HEADS = (D_MODEL // 4) // HEAD_DIM
NA_W = NA_HEADS * HEAD_DIM
NA_ROWS = 8
NA_COLS = 16
N_BRANCH = 3
D_IN = POOL_WIDTH + 3 * DIFF_W + 3 * NA_W + N_BRANCH * D_MODEL
D_FF = ((8 * D_MODEL // 3 + 127) // 128) * 128
N_MOD = 9
ROPE_BASE = 10000.0
LN_EPS = 1e-5
RMS_EPS = 1e-5
ALPHA = (2 * DEPTH) ** 0.25
ATTN_SCALE = HEAD_DIM ** -0.5
NEG_INF = -1e30

OFF_A = 0
OFF_QB = OFF_A + POOL_WIDTH
OFF_KB = OFF_QB + DIFF_W
OFF_VB = OFF_KB + DIFF_W
OFF_QC = OFF_VB + DIFF_W
OFF_KC = OFF_QC + NA_W
OFF_VC = OFF_KC + NA_W
OFF_GT = OFF_VC + NA_W
KT_W = DIFF_W + 2 * NA_W

LANES = 128
MXU_DIM = 256
VMEM_LIMIT = 56 * 1024 * 1024

FF_CHUNK = MXU_DIM
N_FF_CHUNKS = D_FF // FF_CHUNK
TOK_TILE = 512
FFN_TILE = 1024
MIX_TILE = 256
MERGE_TILE = 512
POOL_BLOCK = 256
MOD_ROWS = 16
MOD_COLS = 1024
NA_Q_ROWS = 4
NA_K_ROWS = NA_Q_ROWS + NA_ROWS
NA_GRID_ROWS = DEC_SEQ // GRID_W
NA_TILES = NA_GRID_ROWS // NA_Q_ROWS
NA_CASE_TILES = (0, 1, NA_TILES - 1)
SEQ_PER_TILE = TOK_TILE // SEQ

BF16 = jnp.bfloat16
F32 = jnp.float32


def _dot(a, b):
    return jnp.dot(a, b, preferred_element_type=F32)


def _dot_nt(a, b):
    return lax.dot_general(a, b, (((1,), (1,)), ((), ())), preferred_element_type=F32)


def _sigmoid(x):
    return 1.0 / (1.0 + jnp.exp(-x))


def _layer_norm(y, g, b):
    mu = jnp.mean(y, axis=-1, keepdims=True)
    d = y - mu
    var = jnp.mean(d * d, axis=-1, keepdims=True)
    return d * lax.rsqrt(var + LN_EPS) * g + b


def _split3(x):
    hi = x.astype(BF16)
    r = x - hi.astype(F32)
    mid = r.astype(BF16)
    lo = (r - mid.astype(F32)).astype(BF16)
    return hi, mid, lo


def _resident(shape, index=None):
    index = (0,) * len(shape) if index is None else index
    return pl.BlockSpec(shape, lambda *_: index, pipeline_mode=pl.Buffered(1))


def _layer_block(shape, layer):
    return _resident((1,) + shape, (layer,) + (0,) * len(shape))


def _params(*sem):
    return pltpu.CompilerParams(dimension_semantics=sem, vmem_limit_bytes=VMEM_LIMIT)


def _mod_kernel(cond_ref, w_ref, b_ref, o_ref):
    c = cond_ref[...]
    x = c * _sigmoid(c)
    w = w_ref[0]
    x_hi = x.astype(BF16)
    x_lo = (x - x_hi.astype(F32)).astype(BF16)
    w_hi = w.astype(BF16)
    w_lo = (w - w_hi.astype(F32)).astype(BF16)
    o_ref[0] = _dot(x_hi, w_hi) + _dot(x_lo, w_hi) + _dot(x_hi, w_lo) + b_ref[0]


def _modulation(cond, w_mod, b_mod):
    n_cols = N_MOD * D_MODEL
    return pl.pallas_call(
        _mod_kernel,
        out_shape=jax.ShapeDtypeStruct((DEPTH, MOD_ROWS, n_cols), F32),
        grid=(DEPTH, n_cols // MOD_COLS),
        in_specs=[
            pl.BlockSpec((MOD_ROWS, D_MODEL), lambda l, j: (0, 0)),
            pl.BlockSpec((1, D_MODEL, MOD_COLS), lambda l, j: (l, 0, j)),
            pl.BlockSpec((1, 1, MOD_COLS), lambda l, j: (l, 0, j)),
        ],
        out_specs=pl.BlockSpec((1, MOD_ROWS, MOD_COLS), lambda l, j: (l, 0, j)),
        compiler_params=_params("parallel", "parallel"),
        name="modulation",
    )(cond, w_mod, b_mod.reshape(DEPTH, 1, n_cols))


def _ffn_kernel(x_ref, m_ref, w1_ref, w3_ref, w2_ref, g_ref, b_ref, o_ref, xm_ref, acc_ref, *, mod_idx, ln_idx):
    x = x_ref[...]
    m = m_ref[0, 0]
    shift = m[3 * mod_idx:3 * mod_idx + 1]
    scale = m[3 * mod_idx + 1:3 * mod_idx + 2]
    gate = m[3 * mod_idx + 2:3 * mod_idx + 3]
    xm_ref[...] = (x * (1.0 + scale) + shift).astype(BF16)

    def chunk_out(c):
        cols = pl.ds(pl.multiple_of(c * FF_CHUNK, FF_CHUNK), FF_CHUNK)
        xm = xm_ref[...]
        h1 = _dot(xm, w1_ref[0, :, cols])
        h3 = _dot(xm, w3_ref[0, :, cols])
        act = (h1 * _sigmoid(h1)) * h3
        return _dot(act.astype(BF16), w2_ref[0, cols, :])

    acc_ref[...] = chunk_out(0)

    def chunk(c, carry):
        acc_ref[...] += chunk_out(c)
        return carry

    lax.fori_loop(1, N_FF_CHUNKS, chunk, 0)
    y = ALPHA * x + (0.5 * gate) * acc_ref[...]
    o_ref[...] = _layer_norm(y, g_ref[0, ln_idx:ln_idx + 1], b_ref[0, ln_idx:ln_idx + 1])


def _mod_spec(layer, tiles_per_group):
    return pl.BlockSpec((1, 1, N_MOD, D_MODEL), lambda i: (layer, i // tiles_per_group, 0, 0))


def _ffn(x, mod, w1, w3, w2, ln_g, ln_b, *, layer, mod_idx, ln_idx, tiles_per_group):
    n_tok = x.shape[0]
    return pl.pallas_call(
        functools.partial(_ffn_kernel, mod_idx=mod_idx, ln_idx=ln_idx),
        out_shape=jax.ShapeDtypeStruct((n_tok, D_MODEL), F32),
        grid=(n_tok // FFN_TILE,),
        in_specs=[
            pl.BlockSpec((FFN_TILE, D_MODEL), lambda i: (i, 0)),
            _mod_spec(layer, tiles_per_group),
            _layer_block((D_MODEL, D_FF), layer),
            _layer_block((D_MODEL, D_FF), layer),
            _layer_block((D_FF, D_MODEL), layer),
            _layer_block((3, D_MODEL), layer),
            _layer_block((3, D_MODEL), layer),
        ],
        out_specs=pl.BlockSpec((FFN_TILE, D_MODEL), lambda i: (i, 0)),
        scratch_shapes=[pltpu.VMEM((FFN_TILE, D_MODEL), BF16), pltpu.VMEM((FFN_TILE, D_MODEL), F32)],
        compiler_params=_params("parallel"),
        name="ffn",
    )(x, mod, w1, w3, w2, ln_g, ln_b)


def _rope(x, cos, sin_signed):
    lane = lax.broadcasted_iota(jnp.int32, x.shape, 1)
    first = (lane % (HEAD_DIM // 2)) < (HEAD_DIM // 4)
    partner = jnp.where(first, pltpu.roll(x, LANES - HEAD_DIM // 4, 1), pltpu.roll(x, HEAD_DIM // 4, 1))
    return x * cos + partner * sin_signed


def _modulated(x_ref, m_ref):
    m = m_ref[0, 0]
    return (x_ref[...] * (1.0 + m[4:5]) + m[3:4]).astype(BF16)


def _inproj_ctx_kernel(*refs, slab):
    x_ref, m_ref, w_ref, wt_ref = refs[:4]
    a_ref, qb_ref, kbt_ref, vb_ref, qc_ref, kct_ref, vct_ref, gt_ref = refs[-8:]
    xm = _modulated(x_ref, m_ref)

    def proj(off, width):
        return _dot(xm, w_ref[0, :, off:off + width])

    a_ref[...] = proj(OFF_A, POOL_WIDTH)
    qb_ref[...] = (proj(OFF_QB, DIFF_W) * ATTN_SCALE).astype(qb_ref.dtype)
    qc_ref[...] = (proj(OFF_QC, NA_W) * ATTN_SCALE).astype(qc_ref.dtype)
    v = proj(OFF_VB, DIFF_W)
    for j in range(N_BRANCH):
        gt_ref[:, j * D_MODEL:(j + 1) * D_MODEL] = proj(OFF_GT + j * D_MODEL, D_MODEL)
    zt = _dot_nt(wt_ref[0], xm)
    for s in range(SEQ_PER_TILE):
        tok = slice(s * SEQ, (s + 1) * SEQ)
        vb_ref[s, slab] = v[tok]
        kbt_ref[s, slab] = zt[:DIFF_W, tok]
        kct_ref[s, slab] = zt[DIFF_W:DIFF_W + NA_W, tok]
        vct_ref[s, slab] = zt[DIFF_W + NA_W:, tok]
        for other in range(vb_ref.shape[1]):
            if other != slab:
                for ref in (vb_ref, kbt_ref, kct_ref, vct_ref):
                    ref[s, other] = jnp.zeros(ref.shape[2:], ref.dtype)


def _inproj_ctx(x, mod, w_in, w_in_t, kv_prev, *, layer):
    n_tok = x.shape[0]
    tok = lambda width: pl.BlockSpec((TOK_TILE, width), lambda i: (i, 0))
    if kv_prev is None:
        slab = layer
        kv = lambda rows, cols: pl.BlockSpec((SEQ_PER_TILE, DEPTH, rows, cols), lambda i: (i, 0, 0, 0))
    else:
        slab = 0
        kv = lambda rows, cols: pl.BlockSpec((SEQ_PER_TILE, 1, rows, cols), lambda i: (i, layer, 0, 0))
    kv_shape = lambda rows, cols: jax.ShapeDtypeStruct((BATCH, DEPTH, rows, cols), F32)
    in_specs = [
        tok(D_MODEL),
        _mod_spec(layer, n_tok // TOK_TILE),
        _layer_block((D_MODEL, D_IN), layer),
        _layer_block((KT_W, D_MODEL), layer),
    ]
    args = [x, mod, w_in, w_in_t]
    aliases = {}
    if kv_prev is not None:
        for arr, out_idx in zip(kv_prev, (2, 3, 5, 6)):
            aliases[len(args)] = out_idx
            args.append(arr)
            in_specs.append(pl.BlockSpec(memory_space=pl.ANY))
    return pl.pallas_call(
        functools.partial(_inproj_ctx_kernel, slab=slab),
        out_shape=[
            jax.ShapeDtypeStruct((n_tok, POOL_WIDTH), F32),
            jax.ShapeDtypeStruct((n_tok, DIFF_W), BF16),
            kv_shape(DIFF_W, SEQ),
            kv_shape(SEQ, DIFF_W),
            jax.ShapeDtypeStruct((n_tok, NA_W), BF16),
            kv_shape(NA_W, SEQ),
            kv_shape(NA_W, SEQ),
            jax.ShapeDtypeStruct((n_tok, N_BRANCH * D_MODEL), F32),
        ],
        grid=(n_tok // TOK_TILE,),
        in_specs=in_specs,
        out_specs=[tok(POOL_WIDTH), tok(DIFF_W), kv(DIFF_W, SEQ), kv(SEQ, DIFF_W),
                   tok(NA_W), kv(NA_W, SEQ), kv(NA_W, SEQ), tok(N_BRANCH * D_MODEL)],
        input_output_aliases=aliases,
        compiler_params=_params("parallel"),
        name="inproj_ctx",
    )(*args)


def _inproj_lat_kernel(x_ref, m_ref, w_ref, cos_ref, sin_ref,
                       a_ref, qb_ref, kb_ref, vb_ref, qc_ref, kc_ref, vc_ref, gt_ref):
    xm = _modulated(x_ref, m_ref)

    def proj(off, width):
        return _dot(xm, w_ref[0, :, off:off + width])

    a_ref[...] = proj(OFF_A, POOL_WIDTH)
    for h in range(DIFF_HEADS):
        lo = h * 2 * HEAD_DIM
        q = _rope(proj(OFF_QB + lo, 2 * HEAD_DIM), cos_ref[...], sin_ref[...])
        k = _rope(proj(OFF_KB + lo, 2 * HEAD_DIM), cos_ref[...], sin_ref[...])
        qb_ref[:, lo:lo + 2 * HEAD_DIM] = (q * ATTN_SCALE).astype(qb_ref.dtype)
        kb_ref[:, lo:lo + 2 * HEAD_DIM] = k.astype(kb_ref.dtype)
    vb_ref[...] = proj(OFF_VB, DIFF_W).astype(vb_ref.dtype)
    qc_ref[...] = (proj(OFF_QC, NA_W) * ATTN_SCALE).astype(qc_ref.dtype)
    kc_ref[...] = proj(OFF_KC, NA_W).astype(kc_ref.dtype)
    vc_ref[...] = proj(OFF_VC, NA_W).astype(vc_ref.dtype)
    for j in range(N_BRANCH):
        gt_ref[:, j * D_MODEL:(j + 1) * D_MODEL] = proj(OFF_GT + j * D_MODEL, D_MODEL)


def _inproj_lat(x, mod, w_in, rope_tabs, *, layer):
    n_tok = x.shape[0]
    tok = lambda width: pl.BlockSpec((TOK_TILE, width), lambda i: (i, 0))
    tiles_per_seq = DEC_SEQ // TOK_TILE
    widths = (POOL_WIDTH, DIFF_W, DIFF_W, DIFF_W, NA_W, NA_W, NA_W, N_BRANCH * D_MODEL)
    dtypes = (F32, BF16, BF16, BF16, BF16, BF16, BF16, F32)
    return pl.pallas_call(
        _inproj_lat_kernel,
        out_shape=[jax.ShapeDtypeStruct((n_tok, w), d) for w, d in zip(widths, dtypes)],
        grid=(n_tok // TOK_TILE,),
        in_specs=[
            tok(D_MODEL),
            _mod_spec(layer, tiles_per_seq),
            _layer_block((D_MODEL, D_IN), layer),
            pl.BlockSpec((TOK_TILE, LANES), lambda i: (i % tiles_per_seq, 0)),
            pl.BlockSpec((TOK_TILE, LANES), lambda i: (i % tiles_per_seq, 0)),
        ],
        out_specs=[tok(w) for w in widths],
        compiler_params=_params("parallel"),
        name="inproj_lat",
    )(x, mod, w_in, *rope_tabs)


def _diff_lambda(lq1, lk1, lq2, lk2, lam_init):
    e1 = jnp.exp(jnp.sum(lq1 * lk1, axis=-1, keepdims=True))
    e2 = jnp.exp(jnp.sum(lq2 * lk2, axis=-1, keepdims=True))
    return e1 - e2 + lam_init


def _map_masks(q):
    lane = lax.broadcasted_iota(jnp.int32, q.shape, 1)
    zero = jnp.zeros_like(q)
    return jnp.where(lane < HEAD_DIM, q, zero), jnp.where(lane >= HEAD_DIM, q, zero)


def _head_mask(shape, h):
    lane = lax.broadcasted_iota(jnp.int32, shape, 1)
    return (lane >= h * HEAD_DIM) & (lane < (h + 1) * HEAD_DIM)


def _sub_ln(o, g, lam_init):
    return o * lax.rsqrt(jnp.mean(o * o, axis=-1, keepdims=True) + RMS_EPS) * g * (1.0 - lam_init)


_LAM_SPECS = lambda layer: [_layer_block((1, HEAD_DIM), layer)] * 4 + [_layer_block((1, 2 * HEAD_DIM), layer)]


def _lam_args(lam_q1, lam_k1, lam_q2, lam_k2, subln_g):
    vec = lambda a: a.reshape(DEPTH, 1, a.shape[-1])
    return vec(lam_q1), vec(lam_k1), vec(lam_q2), vec(lam_k2), vec(subln_g)


def _ctx_attn_kernel(qb_ref, kbt_ref, vb_ref, qc_ref, kct_ref, vct_ref,
                     lq1_ref, lk1_ref, lq2_ref, lk2_ref, g_ref, yb_ref, yc_ref, *, lam_init):
    lam = _diff_lambda(lq1_ref[0], lk1_ref[0], lq2_ref[0], lk2_ref[0], lam_init)
    for h in range(DIFF_HEADS):
        sl = slice(h * 2 * HEAD_DIM, (h + 1) * 2 * HEAD_DIM)
        q0, q1 = _map_masks(qb_ref[:, sl])
        kt = kbt_ref[0, 0, sl, :].astype(BF16)
        v = vb_ref[0, 0, :, sl].astype(BF16)
        probs = []
        for q in (q0, q1):
            s = _dot(q, kt)
            e = jnp.exp(s - jnp.max(s, axis=-1, keepdims=True))
            probs.append(e * (1.0 / jnp.sum(e, axis=-1, keepdims=True)))
        a = (probs[0] - lam * probs[1]).astype(BF16)
        yb_ref[:, sl] = _sub_ln(_dot(a, v), g_ref[0], lam_init).astype(yb_ref.dtype)

    qc = qc_ref[...]
    kct = kct_ref[0, 0].astype(BF16)
    vct = vct_ref[0, 0].astype(BF16)
    out = jnp.zeros(qc.shape, F32)
    for h in range(NA_HEADS):
        mask = _head_mask(qc.shape, h)
        s = _dot(jnp.where(mask, qc, jnp.zeros_like(qc)), kct)
        e = jnp.exp(s - jnp.max(s, axis=-1, keepdims=True))
        p = (e * (1.0 / jnp.sum(e, axis=-1, keepdims=True))).astype(BF16)
        out = out + jnp.where(mask, _dot_nt(p, vct), 0.0)
    yc_ref[...] = out.astype(yc_ref.dtype)


def _ctx_attention(qb, kbt, vb, qc, kct, vct, lam_args, *, layer, lam_init):
    n_tok = qb.shape[0]
    seq = lambda width: pl.BlockSpec((SEQ, width), lambda b: (b, 0))
    kv = lambda rows, cols: pl.BlockSpec((1, 1, rows, cols), lambda b: (b, layer, 0, 0))
    return pl.pallas_call(
        functools.partial(_ctx_attn_kernel, lam_init=lam_init),
        out_shape=[jax.ShapeDtypeStruct((n_tok, DIFF_W), BF16), jax.ShapeDtypeStruct((n_tok, NA_W), BF16)],
        grid=(n_tok // SEQ,),
        in_specs=[seq(DIFF_W), kv(DIFF_W, SEQ), kv(SEQ, DIFF_W), seq(NA_W), kv(NA_W, SEQ), kv(NA_W, SEQ)]
                 + _LAM_SPECS(layer),
        out_specs=[seq(DIFF_W), seq(NA_W)],
        compiler_params=_params("parallel"),
        name="ctx_attention",
    )(qb, kbt, vb, qc, kct, vct, *lam_args)


def _lat_diff_kernel(q_ref, k_ref, v_ref, ckt_ref, cv_ref,
                     lq1_ref, lk1_ref, lq2_ref, lk2_ref, g_ref, y_ref, *, lam_init):
    lam = _diff_lambda(lq1_ref[0], lk1_ref[0], lq2_ref[0], lk2_ref[0], lam_init)
    for h in range(DIFF_HEADS):
        sl = slice(h * 2 * HEAD_DIM, (h + 1) * 2 * HEAD_DIM)
        q0, q1 = _map_masks(q_ref[0, :, sl])
        k = k_ref[0, :, sl]
        v = v_ref[0, :, sl]
        ckt = ckt_ref[0, 0, sl, :].astype(BF16)
        cv = cv_ref[0, 0, :, sl].astype(BF16)
        parts = []
        for q in (q0, q1):
            s_new = _dot_nt(q, k)
            s_old = _dot(q, ckt)
            top = jnp.maximum(jnp.max(s_new, axis=-1, keepdims=True), jnp.max(s_old, axis=-1, keepdims=True))
            e_new = jnp.exp(s_new - top)
            e_old = jnp.exp(s_old - top)
            inv = 1.0 / (jnp.sum(e_new, axis=-1, keepdims=True) + jnp.sum(e_old, axis=-1, keepdims=True))
            parts.append((e_new, e_old, inv))
        (n0, o0, r0), (n1, o1, r1) = parts
        r1 = lam * r1
        a_new = (n0 * r0 - n1 * r1).astype(BF16)
        a_old = (o0 * r0 - o1 * r1).astype(BF16)
        o = _dot(a_new, v) + _dot(a_old, cv)
        y_ref[0, :, sl] = _sub_ln(o, g_ref[0], lam_init).astype(y_ref.dtype)


def _lat_diff_attention(qb, kb, vb, cache_kt, cache_v, lam_args, *, layer, lam_init):
    full = pl.BlockSpec((1, DEC_SEQ, DIFF_W), lambda b, i: (b, 0, 0))
    past = lambda rows, cols: pl.BlockSpec((1, 1, rows, cols), lambda b, i: (b, layer, 0, 0))
    tile = pl.BlockSpec((1, MIX_TILE, DIFF_W), lambda b, i: (b, i, 0))
    return pl.pallas_call(
        functools.partial(_lat_diff_kernel, lam_init=lam_init),
        out_shape=jax.ShapeDtypeStruct((DEC_BATCH, DEC_SEQ, DIFF_W), BF16),
        grid=(DEC_BATCH, DEC_SEQ // MIX_TILE),
        in_specs=[tile, full, full, past(DIFF_W, PAST_LEN), past(PAST_LEN, DIFF_W)] + _LAM_SPECS(layer),
        out_specs=tile,
        compiler_params=_params("parallel", "parallel"),
        name="latent_diff_attention",
    )(qb, kb, vb, cache_kt, cache_v, *lam_args)


def _na_window_start(tile):
    return jnp.clip(tile * NA_Q_ROWS - NA_ROWS // 2, 0, NA_GRID_ROWS - NA_K_ROWS)


def _toeplitz(v):
    n = GRID_W
    period = 2 * n - 1
    reps = [1] * (v.ndim - 1) + [n + 1]
    flat = jnp.tile(v, reps)[..., :n * 2 * n]
    skew = flat.reshape(*v.shape[:-1], n, 2 * n)
    return skew[..., ::-1, :n]


def _na_bias_table(rpb):
    edge = GRID_W - NA_COLS
    ext = jnp.concatenate([jnp.repeat(rpb[..., :1], edge, axis=-1), rpb,
                           jnp.repeat(rpb[..., -1:], edge, axis=-1)], axis=-1)
    blocks = _toeplitz(ext)
    col = jnp.arange(GRID_W)
    col_start = jnp.clip(col - NA_COLS // 2, 0, GRID_W - NA_COLS)
    col_ok = (col[None, :] >= col_start[:, None]) & (col[None, :] < col_start[:, None] + NA_COLS)
    blocks = jnp.where(col_ok, blocks, NEG_INF)
    masked = jnp.full((NA_HEADS, GRID_W, GRID_W), NEG_INF, F32)
    tables = []
    for tile in NA_CASE_TILES:
        k0 = min(max(tile * NA_Q_ROWS - NA_ROWS // 2, 0), NA_GRID_ROWS - NA_K_ROWS)
        q_rows = []
        for qr in range(tile * NA_Q_ROWS, (tile + 1) * NA_Q_ROWS):
            row_start = min(max(qr - NA_ROWS // 2, 0), NA_GRID_ROWS - NA_ROWS)
            parts = []
            for kr in range(k0, k0 + NA_K_ROWS):
                inside = row_start <= kr < row_start + NA_ROWS
                parts.append(blocks[:, kr - qr + NA_ROWS - 1] if inside else masked)
            q_rows.append(jnp.concatenate(parts, axis=-1))
        tables.append(jnp.concatenate(q_rows, axis=1))
    return jnp.stack(tables).astype(F32)


def _lat_na_kernel(q_ref, k_ref, v_ref, ckt_ref, cvt_ref, bias_ref, y_ref):
    start = _na_window_start(pl.program_id(1))
    win = pl.ds(pl.multiple_of(start * GRID_W, NA_Q_ROWS * GRID_W), NA_K_ROWS * GRID_W)
    k = k_ref[0, win, :]
    v = v_ref[0, win, :]
    ckt = ckt_ref[0, 0].astype(BF16)
    cvt = cvt_ref[0, 0].astype(BF16)
    q = q_ref[0]
    out = jnp.zeros(q.shape, F32)
    for h in range(NA_HEADS):
        mask = _head_mask(q.shape, h)
        qh = jnp.where(mask, q, jnp.zeros_like(q))
        s_nb = _dot_nt(qh, k) + bias_ref[0, h]
        s_ctx = _dot(qh, ckt)
        top = jnp.maximum(jnp.max(s_nb, axis=-1, keepdims=True), jnp.max(s_ctx, axis=-1, keepdims=True))
        e_nb = jnp.exp(s_nb - top)
        e_ctx = jnp.exp(s_ctx - top)
        inv = 1.0 / (jnp.sum(e_nb, axis=-1, keepdims=True) + jnp.sum(e_ctx, axis=-1, keepdims=True))
        o = _dot((e_nb * inv).astype(BF16), v) + _dot_nt((e_ctx * inv).astype(BF16), cvt)
        out = out + jnp.where(mask, o, 0.0)
    y_ref[0] = out.astype(y_ref.dtype)


def _lat_na_attention(qc, kc, vc, cache_kt, cache_vt, bias, *, layer):
    q_tok = NA_Q_ROWS * GRID_W

    def bias_index(b, i):
        return (jnp.where(i == 0, 0, jnp.where(i == NA_TILES - 1, 2, 1)), 0, 0, 0)

    full = pl.BlockSpec((1, DEC_SEQ, NA_W), lambda b, i: (b, 0, 0))
    past = pl.BlockSpec((1, 1, NA_W, PAST_LEN), lambda b, i: (b, layer, 0, 0))
    tile = pl.BlockSpec((1, q_tok, NA_W), lambda b, i: (b, i, 0))
    return pl.pallas_call(
        _lat_na_kernel,
        out_shape=jax.ShapeDtypeStruct((DEC_BATCH, DEC_SEQ, NA_W), BF16),
        grid=(DEC_BATCH, NA_TILES),
        in_specs=[tile, full, full, past, past,
                  pl.BlockSpec((1, NA_HEADS, q_tok, NA_K_ROWS * GRID_W), bias_index)],
        out_specs=tile,
        compiler_params=_params("parallel", "parallel"),
        name="latent_na_attention",
    )(qc, kc, vc, cache_kt, cache_vt, bias)


def _pool(a_ref, pool_w_ref, pool_scale, seg_len):
    assert POOL_WINDOWS == tuple(2 << g for g in range(POOL_GROUPS))
    shape = (POOL_BLOCK, POOL_WIDTH)
    pos = lax.broadcasted_iota(jnp.int32, shape, 0) & (seg_len - 1)
    group = lax.broadcasted_iota(jnp.int32, shape, 1) // POOL_GROUP_W
    half = jnp.left_shift(1, group)
    count = (jnp.minimum(pos, half) + jnp.minimum(seg_len - pos, half)).astype(F32)

    def from_above(x, k):
        return jnp.where(pos >= k, pltpu.roll(x, k, 0), 0.0)

    def from_below(x, k):
        return jnp.where(pos < seg_len - k, pltpu.roll(x, POOL_BLOCK - k, 0), 0.0)

    blocks = []
    for r in range(MERGE_TILE // POOL_BLOCK):
        x = a_ref[r * POOL_BLOCK:(r + 1) * POOL_BLOCK, :]
        trail, lead = [x], [x]
        for g in range(1, POOL_GROUPS):
            k = 1 << (g - 1)
            trail.append(trail[-1] + from_above(trail[-1], k))
            lead.append(lead[-1] + from_below(lead[-1], k))
        total = None
        for g in range(POOL_GROUPS):
            window = from_above(trail[g], 1) + lead[g]
            total = window if total is None else jnp.where(group == g, window, total)
        p = total / count - x
        blocks.append(_dot(p.astype(BF16), pool_w_ref[0]) * pool_scale)
    return jnp.concatenate(blocks, axis=0)


def _merge_kernel(x_ref, m_ref, a_ref, yb_ref, yc_ref, gt_ref, pool_w_ref, pool_scale_ref,
                  wpa_ref, wpb_ref, wpc_ref, wout_ref, g_ref, b_ref, o_ref, *, seg_len):
    y_a = _pool(a_ref, pool_w_ref, pool_scale_ref[0], seg_len)
    branches = (
        _dot(y_a.astype(BF16), wpa_ref[0]),
        _dot(yb_ref[...], wpb_ref[0]),
        _dot(yc_ref[...], wpc_ref[0]),
    )
    mixed = None
    for j, br in enumerate(branches):
        term = _sigmoid(gt_ref[:, j * D_MODEL:(j + 1) * D_MODEL]) * br
        mixed = term if mixed is None else mixed + term
    h = _dot(mixed.astype(BF16), wout_ref[0])
    m = m_ref[0, 0]
    y = ALPHA * x_ref[...] + m[5:6] * h
    o_ref[...] = _layer_norm(y, g_ref[0, 1:2], b_ref[0, 1:2])


def _merge(x, mod, a, y_b, y_c, gt, pool_w, pool_scale, w_pa, w_pb, w_pc, w_out, ln_g, ln_b,
           *, layer, seg_len, tiles_per_group):
    n_tok = x.shape[0]
    tok = lambda width: pl.BlockSpec((MERGE_TILE, width), lambda i: (i, 0))
    return pl.pallas_call(
        functools.partial(_merge_kernel, seg_len=seg_len),
        out_shape=jax.ShapeDtypeStruct((n_tok, D_MODEL), F32),
        grid=(n_tok // MERGE_TILE,),
        in_specs=[
            tok(D_MODEL),
            _mod_spec(layer, tiles_per_group),
            tok(POOL_WIDTH), tok(DIFF_W), tok(NA_W), tok(N_BRANCH * D_MODEL),
            _layer_block((POOL_WIDTH, POOL_WIDTH), layer),
            _layer_block((1, POOL_WIDTH), layer),
            _layer_block((POOL_WIDTH, D_MODEL), layer),
            _layer_block((DIFF_W, D_MODEL), layer),
            _layer_block((NA_W, D_MODEL), layer),
            _layer_block((D_MODEL, D_MODEL), layer),
            _layer_block((3, D_MODEL), layer),
            _layer_block((3, D_MODEL), layer),
        ],
        out_specs=tok(D_MODEL),
        compiler_params=_params("parallel"),
        name="merge",
    )(x, mod, a, y_b, y_c, gt, pool_w, pool_scale, w_pa, w_pb, w_pc, w_out, ln_g, ln_b)


def _rope_tables():
    t = jnp.arange(DEC_SEQ)
    row = (t // GRID_W).astype(F32)
    col = (t % GRID_W).astype(F32)
    nf = HEAD_DIM // 4
    inv = ROPE_BASE ** (-jnp.arange(nf, dtype=F32) / nf)
    ang_row = row[:, None] * inv
    ang_col = col[:, None] * inv

    def half(ang):
        return (jnp.concatenate([jnp.cos(ang), jnp.cos(ang)], axis=-1),
                jnp.concatenate([-jnp.sin(ang), jnp.sin(ang)], axis=-1))

    cr, sr = half(ang_row)
    cc, sc = half(ang_col)
    cos = jnp.concatenate([cr, cc, cr, cc], axis=-1)
    sin = jnp.concatenate([sr, sc, sr, sc], axis=-1)
    return cos, sin


def kernel(x_prompt, x_sample, cache_diff_k, cache_diff_v, cache_na_k, cache_na_v, c, c_ctx, w_mod, b_mod, ln_g, ln_b, ffn1_w1, ffn1_w3, ffn1_w2, ffn2_w1, ffn2_w3, ffn2_w2, w_in, pool_w, pool_scale, w_pa, w_pb, w_pc, lam_q1, lam_k1, lam_q2, lam_k2, subln_g, na_rpb, w_out):
    cond = jnp.concatenate(
        [c_ctx[None, :], c, jnp.zeros((MOD_ROWS - 1 - DEC_BATCH, D_MODEL), F32)], axis=0)
    mod = _modulation(cond, w_mod, b_mod)
    mod = mod[:, :1 + DEC_BATCH].reshape(DEPTH, 1 + DEC_BATCH, N_MOD, D_MODEL)
    mod_ctx, mod_lat = mod[:, :1], mod[:, 1:]

    cache_dkt = cache_diff_k.transpose(0, 1, 3, 4, 5, 2).reshape(DEC_BATCH, DEPTH, DIFF_W, PAST_LEN)
    cache_dv = cache_diff_v.reshape(DEC_BATCH, DEPTH, PAST_LEN, DIFF_W)
    cache_nkt = cache_na_k.transpose(0, 1, 3, 4, 2).reshape(DEC_BATCH, DEPTH, NA_W, PAST_LEN)
    cache_nvt = cache_na_v.transpose(0, 1, 3, 4, 2).reshape(DEC_BATCH, DEPTH, NA_W, PAST_LEN)
    rope_tabs = _rope_tables()

    ffn1 = (ffn1_w1.astype(BF16), ffn1_w3.astype(BF16), ffn1_w2.astype(BF16))
    ffn2 = (ffn2_w1.astype(BF16), ffn2_w3.astype(BF16), ffn2_w2.astype(BF16))
    w_in_b = w_in.astype(BF16)
    w_in_t = jnp.concatenate([w_in[:, :, OFF_KB:OFF_KB + DIFF_W], w_in[:, :, OFF_KC:OFF_KC + 2 * NA_W]],
                             axis=-1).transpose(0, 2, 1).astype(BF16)
    group_eye = jnp.eye(POOL_GROUPS, dtype=F32)[None, :, None, :, None]
    pool_bd = (pool_w[:, :, :, None, :] * group_eye).reshape(DEPTH, POOL_WIDTH, POOL_WIDTH)
    merge_w = (pool_bd.astype(BF16), pool_scale.reshape(DEPTH, 1, POOL_WIDTH), w_pa.astype(BF16),
               w_pb.astype(BF16), w_pc.astype(BF16), w_out.astype(BF16))
    lam_args = _lam_args(lam_q1, lam_k1, lam_q2, lam_k2, subln_g)

    x_ctx = x_prompt.reshape(BATCH * SEQ, D_MODEL)
    x_lat = x_sample.reshape(DEC_BATCH * DEC_SEQ, D_MODEL)
    ctx_tiles = (BATCH * SEQ) // FFN_TILE
    lat_tiles = DEC_SEQ // FFN_TILE
    kv_ctx = None

    for l in range(DEPTH):
        lam_init = 0.8 - 0.6 * math.exp(-0.3 * l)

        x_ctx = _ffn(x_ctx, mod_ctx, *ffn1, ln_g, ln_b, layer=l, mod_idx=0, ln_idx=0, tiles_per_group=ctx_tiles)
        a, qb, kbt, vb, qc, kct, vct, gt = _inproj_ctx(x_ctx, mod_ctx, w_in_b, w_in_t, kv_ctx, layer=l)
        kv_ctx = (kbt, vb, kct, vct)
        y_b, y_c = _ctx_attention(qb, kbt, vb, qc, kct, vct, lam_args, layer=l, lam_init=lam_init)
        x_ctx = _merge(x_ctx, mod_ctx, a, y_b, y_c, gt, *merge_w, ln_g, ln_b, layer=l,
                       seg_len=SEQ, tiles_per_group=(BATCH * SEQ) // MERGE_TILE)
        x_ctx = _ffn(x_ctx, mod_ctx, *ffn2, ln_g, ln_b, layer=l, mod_idx=2, ln_idx=2, tiles_per_group=ctx_tiles)

        x_lat = _ffn(x_lat, mod_lat, *ffn1, ln_g, ln_b, layer=l, mod_idx=0, ln_idx=0, tiles_per_group=lat_tiles)
        a, qb, kb, vb, qc, kc, vc, gt = _inproj_lat(x_lat, mod_lat, w_in_b, rope_tabs, layer=l)
        seq3 = lambda z: z.reshape(DEC_BATCH, DEC_SEQ, z.shape[-1])
        y_b = _lat_diff_attention(seq3(qb), seq3(kb), seq3(vb), cache_dkt, cache_dv, lam_args,
                                  layer=l, lam_init=lam_init)
        y_c = _lat_na_attention(seq3(qc), seq3(kc), seq3(vc), cache_nkt, cache_nvt,
                                _na_bias_table(na_rpb[l]), layer=l)
        x_lat = _merge(x_lat, mod_lat, a, y_b.reshape(-1, DIFF_W), y_c.reshape(-1, NA_W), gt, *merge_w,
                       ln_g, ln_b, layer=l, seg_len=GRID_W, tiles_per_group=DEC_SEQ // MERGE_TILE)
        x_lat = _ffn(x_lat, mod_lat, *ffn2, ln_g, ln_b, layer=l, mod_idx=2, ln_idx=2, tiles_per_group=lat_tiles)

    kbt, vb, kct, vct = kv_ctx
    new_diff_k = kbt.reshape(BATCH, DEPTH, DIFF_HEADS, 2, HEAD_DIM, SEQ).transpose(0, 1, 5, 2, 3, 4)
    new_diff_v = vb.reshape(BATCH, DEPTH, SEQ, DIFF_HEADS, 2 * HEAD_DIM)
    new_na_k = kct.reshape(BATCH, DEPTH, NA_HEADS, HEAD_DIM, SEQ).transpose(0, 1, 4, 2, 3)
    new_na_v = vct.reshape(BATCH, DEPTH, NA_HEADS, HEAD_DIM, SEQ).transpose(0, 1, 4, 2, 3)
    return (x_ctx.reshape(BATCH, SEQ, D_MODEL), x_lat.reshape(DEC_BATCH, DEC_SEQ, D_MODEL),
            new_diff_k, new_diff_v, new_na_k, new_na_v)
```

```python
import functools
import math

import jax
import jax.numpy as jnp
from jax import lax
from jax.experimental import pallas as pl
from jax.experimental.pallas import tpu as pltpu

D_MODEL = 1024
BATCH = 32
SEQ = 256
DEPTH = 2
DEC_BATCH = 2
DEC_SEQ = 2048
PAST_LEN = 512

GRID_W = 64
HEAD_DIM = 64
POOL_WIDTH = D_MODEL // 4
POOL_GROUPS = 4
POOL_GROUP_W = POOL_WIDTH // POOL_GROUPS
POOL_WINDOWS = (2, 4, 8, 16)
DIFF_HEADS = (D_MODEL // 2) // (2 * HEAD_DIM)
DIFF_W = DIFF_HEADS * 2 * HEAD_DIM
NA_HEADS = (D_MODEL // 4) // HEAD_DIM
NA_W = NA_HEADS * HEAD_DIM
NA_ROWS = 8
NA_COLS = 16
N_BRANCH = 3
D_IN = POOL_WIDTH + 3 * DIFF_W + 3 * NA_W + N_BRANCH * D_MODEL
D_FF = ((8 * D_MODEL // 3 + 127) // 128) * 128
N_MOD = 9
ROPE_BASE = 10000.0
LN_EPS = 1e-5
RMS_EPS = 1e-5
ALPHA = (2 * DEPTH) ** 0.25
ATTN_SCALE = HEAD_DIM ** -0.5
NEG_INF = -1e30

OFF_A = 0
OFF_QB = OFF_A + POOL_WIDTH
OFF_KB = OFF_QB + DIFF_W
OFF_VB = OFF_KB + DIFF_W
OFF_QC = OFF_VB + DIFF_W
OFF_KC = OFF_QC + NA_W
OFF_VC = OFF_KC + NA_W
OFF_GT = OFF_VC + NA_W
KT_W = DIFF_W + 2 * NA_W

LANES = 128
MXU_DIM = 256
VMEM_LIMIT = 56 * 1024 * 1024

FF_CHUNK = MXU_DIM
N_FF_CHUNKS = D_FF // FF_CHUNK
TOK_TILE = 512
FFN_TILE = 1024
MIX_TILE = 256
MERGE_TILE = 512
POOL_BLOCK = 256
MOD_ROWS = 16
MOD_COLS = 1024
NA_Q_ROWS = 4
NA_K_ROWS = NA_Q_ROWS + NA_ROWS
NA_GRID_ROWS = DEC_SEQ // GRID_W
NA_TILES = NA_GRID_ROWS // NA_Q_ROWS
NA_CASE_TILES = (0, 1, NA_TILES - 1)
SEQ_PER_TILE = TOK_TILE // SEQ

BF16 = jnp.bfloat16
F32 = jnp.float32


def _dot(a, b):
    return jnp.dot(a, b, preferred_element_type=F32)


def _dot_nt(a, b):
    return lax.dot_general(a, b, (((1,), (1,)), ((), ())), preferred_element_type=F32)


def _sigmoid(x):
    return 1.0 / (1.0 + jnp.exp(-x))


def _layer_norm(y, g, b):
    mu = jnp.mean(y, axis=-1, keepdims=True)
    d = y - mu
    var = jnp.mean(d * d, axis=-1, keepdims=True)
    return d * lax.rsqrt(var + LN_EPS) * g + b


def _resident(shape, index=None):
    index = (0,) * len(shape) if index is None else index
    return pl.BlockSpec(shape, lambda *_: index, pipeline_mode=pl.Buffered(1))


def _layer_block(shape, layer):
    return _resident((1,) + shape, (layer,) + (0,) * len(shape))


def _params(*sem):
    return pltpu.CompilerParams(dimension_semantics=sem, vmem_limit_bytes=VMEM_LIMIT)


def _mod_kernel(cond_ref, w_ref, b_ref, o_ref):
    c = cond_ref[...]
    x = c * _sigmoid(c)
    w = w_ref[0]
    x_hi = x.astype(BF16)
    x_lo = (x - x_hi.astype(F32)).astype(BF16)
    w_hi = w.astype(BF16)
    w_lo = (w - w_hi.astype(F32)).astype(BF16)
    o_ref[0] = _dot(x_hi, w_hi) + _dot(x_lo, w_hi) + _dot(x_hi, w_lo) + b_ref[0]


def _modulation(cond, w_mod, b_mod):
    n_cols = N_MOD * D_MODEL
    return pl.pallas_call(
        _mod_kernel,
        out_shape=jax.ShapeDtypeStruct((DEPTH, MOD_ROWS, n_cols), F32),
        grid=(DEPTH, n_cols // MOD_COLS),
        in_specs=[
            pl.BlockSpec((MOD_ROWS, D_MODEL), lambda l, j: (0, 0)),
            pl.BlockSpec((1, D_MODEL, MOD_COLS), lambda l, j: (l, 0, j)),
            pl.BlockSpec((1, 1, MOD_COLS), lambda l, j: (l, 0, j)),
        ],
        out_specs=pl.BlockSpec((1, MOD_ROWS, MOD_COLS), lambda l, j: (l, 0, j)),
        compiler_params=_params("parallel", "parallel"),
        name="modulation",
    )(cond, w_mod, b_mod.reshape(DEPTH, 1, n_cols))


def _ffn_kernel(x_ref, m_ref, w1_ref, w3_ref, w2_ref, g_ref, b_ref, o_ref, xm_ref, acc_ref, *, mod_idx, ln_idx):
    x = x_ref[...]
    m = m_ref[0, 0]
    shift = m[3 * mod_idx:3 * mod_idx + 1]
    scale = m[3 * mod_idx + 1:3 * mod_idx + 2]
    gate = m[3 * mod_idx + 2:3 * mod_idx + 3]
    xm_ref[...] = (x * (1.0 + scale) + shift).astype(BF16)

    def chunk_out(c):
        cols = pl.ds(pl.multiple_of(c * FF_CHUNK, FF_CHUNK), FF_CHUNK)
        xm = xm_ref[...]
        h1 = _dot(xm, w1_ref[0, :, cols])
        h3 = _dot(xm, w3_ref[0, :, cols])
        act = (h1 * _sigmoid(h1)) * h3
        return _dot(act.astype(BF16), w2_ref[0, cols, :])

    acc_ref[...] = chunk_out(0)

    def chunk(c, carry):
        acc_ref[...] += chunk_out(c)
        return carry

    lax.fori_loop(1, N_FF_CHUNKS, chunk, 0, unroll=True)
    y = ALPHA * x + (0.5 * gate) * acc_ref[...]
    o_ref[...] = _layer_norm(y, g_ref[0, ln_idx:ln_idx + 1], b_ref[0, ln_idx:ln_idx + 1])


def _mod_spec(layer, tiles_per_group):
    return pl.BlockSpec((1, 1, N_MOD, D_MODEL), lambda i: (layer, i // tiles_per_group, 0, 0))


def _ffn(x, mod, w1, w3, w2, ln_g, ln_b, *, layer, mod_idx, ln_idx, tiles_per_group):
    n_tok = x.shape[0]
    return pl.pallas_call(
        functools.partial(_ffn_kernel, mod_idx=mod_idx, ln_idx=ln_idx),
        out_shape=jax.ShapeDtypeStruct((n_tok, D_MODEL), F32),
        grid=(n_tok // FFN_TILE,),
        in_specs=[
            pl.BlockSpec((FFN_TILE, D_MODEL), lambda i: (i, 0)),
            _mod_spec(layer, tiles_per_group),
            _layer_block((D_MODEL, D_FF), layer),
            _layer_block((D_MODEL, D_FF), layer),
            _layer_block((D_FF, D_MODEL), layer),
            _layer_block((3, D_MODEL), layer),
            _layer_block((3, D_MODEL), layer),
        ],
        out_specs=pl.BlockSpec((FFN_TILE, D_MODEL), lambda i: (i, 0)),
        scratch_shapes=[pltpu.VMEM((FFN_TILE, D_MODEL), BF16), pltpu.VMEM((FFN_TILE, D_MODEL), F32)],
        compiler_params=_params("parallel"),
        name="ffn",
    )(x, mod, w1, w3, w2, ln_g, ln_b)


def _rope(x, cos, sin_signed):
    lane = lax.broadcasted_iota(jnp.int32, x.shape, 1)
    first = (lane % (HEAD_DIM // 2)) < (HEAD_DIM // 4)
    partner = jnp.where(first, pltpu.roll(x, LANES - HEAD_DIM // 4, 1), pltpu.roll(x, HEAD_DIM // 4, 1))
    return x * cos + partner * sin_signed


def _modulated(x_ref, m_ref):
    m = m_ref[0, 0]
    return (x_ref[...] * (1.0 + m[4:5]) + m[3:4]).astype(BF16)


def _inproj_ctx_kernel(*refs, slab):
    x_ref, m_ref, w_ref, wt_ref = refs[:4]
    a_ref, qb_ref, kbt_ref, vb_ref, qc_ref, kct_ref, vct_ref, gt_ref = refs[-8:]
    xm = _modulated(x_ref, m_ref)

    def proj(off, width):
        return _dot(xm, w_ref[0, :, off:off + width])

    a_ref[...] = proj(OFF_A, POOL_WIDTH)
    qb_ref[...] = (proj(OFF_QB, DIFF_W) * ATTN_SCALE).astype(qb_ref.dtype)
    qc_ref[...] = (proj(OFF_QC, NA_W) * ATTN_SCALE).astype(qc_ref.dtype)
    v = proj(OFF_VB, DIFF_W)
    for j in range(N_BRANCH):
        gt_ref[:, j * D_MODEL:(j + 1) * D_MODEL] = proj(OFF_GT + j * D_MODEL, D_MODEL)
    zt = _dot_nt(wt_ref[0], xm)
    for s in range(SEQ_PER_TILE):
        tok = slice(s * SEQ, (s + 1) * SEQ)
        vb_ref[s, slab] = v[tok]
        kbt_ref[s, slab] = zt[:DIFF_W, tok]
        kct_ref[s, slab] = zt[DIFF_W:DIFF_W + NA_W, tok]
        vct_ref[s, slab] = zt[DIFF_W + NA_W:, tok]
        for other in range(vb_ref.shape[1]):
            if other != slab:
                for ref in (vb_ref, kbt_ref, kct_ref, vct_ref):
                    ref[s, other] = jnp.zeros(ref.shape[2:], ref.dtype)


def _inproj_ctx(x, mod, w_in, w_in_t, kv_prev, *, layer):
    n_tok = x.shape[0]
    tok = lambda width: pl.BlockSpec((TOK_TILE, width), lambda i: (i, 0))
    if kv_prev is None:
        slab = layer
        kv = lambda rows, cols: pl.BlockSpec((SEQ_PER_TILE, DEPTH, rows, cols), lambda i: (i, 0, 0, 0))
    else:
        slab = 0
        kv = lambda rows, cols: pl.BlockSpec((SEQ_PER_TILE, 1, rows, cols), lambda i: (i, layer, 0, 0))
    kv_shape = lambda rows, cols: jax.ShapeDtypeStruct((BATCH, DEPTH, rows, cols), F32)
    in_specs = [
        tok(D_MODEL),
        _mod_spec(layer, n_tok // TOK_TILE),
        _layer_block((D_MODEL, D_IN), layer),
        _layer_block((KT_W, D_MODEL), layer),
    ]
    args = [x, mod, w_in, w_in_t]
    aliases = {}
    if kv_prev is not None:
        for arr, out_idx in zip(kv_prev, (2, 3, 5, 6)):
            aliases[len(args)] = out_idx
            args.append(arr)
            in_specs.append(pl.BlockSpec(memory_space=pl.ANY))
    return pl.pallas_call(
        functools.partial(_inproj_ctx_kernel, slab=slab),
        out_shape=[
            jax.ShapeDtypeStruct((n_tok, POOL_WIDTH), F32),
            jax.ShapeDtypeStruct((n_tok, DIFF_W), BF16),
            kv_shape(DIFF_W, SEQ),
            kv_shape(SEQ, DIFF_W),
            jax.ShapeDtypeStruct((n_tok, NA_W), BF16),
            kv_shape(NA_W, SEQ),
            kv_shape(NA_W, SEQ),
            jax.ShapeDtypeStruct((n_tok, N_BRANCH * D_MODEL), F32),
        ],
        grid=(n_tok // TOK_TILE,),
        in_specs=in_specs,
        out_specs=[tok(POOL_WIDTH), tok(DIFF_W), kv(DIFF_W, SEQ), kv(SEQ, DIFF_W),
                   tok(NA_W), kv(NA_W, SEQ), kv(NA_W, SEQ), tok(N_BRANCH * D_MODEL)],
        input_output_aliases=aliases,
        compiler_params=_params("parallel"),
        name="inproj_ctx",
    )(*args)


def _inproj_lat_kernel(x_ref, m_ref, w_ref, cos_ref, sin_ref,
                       a_ref, qb_ref, kb_ref, vb_ref, qc_ref, kc_ref, vc_ref, gt_ref):
    xm = _modulated(x_ref, m_ref)

    def proj(off, width):
        return _dot(xm, w_ref[0, :, off:off + width])

    a_ref[...] = proj(OFF_A, POOL_WIDTH)
    for h in range(DIFF_HEADS):
        lo = h * 2 * HEAD_DIM
        q = _rope(proj(OFF_QB + lo, 2 * HEAD_DIM), cos_ref[...], sin_ref[...])
        k = _rope(proj(OFF_KB + lo, 2 * HEAD_DIM), cos_ref[...], sin_ref[...])
        qb_ref[:, lo:lo + 2 * HEAD_DIM] = (q * ATTN_SCALE).astype(qb_ref.dtype)
        kb_ref[:, lo:lo + 2 * HEAD_DIM] = k.astype(kb_ref.dtype)
    vb_ref[...] = proj(OFF_VB, DIFF_W).astype(vb_ref.dtype)
    qc_ref[...] = (proj(OFF_QC, NA_W) * ATTN_SCALE).astype(qc_ref.dtype)
    kc_ref[...] = proj(OFF_KC, NA_W).astype(kc_ref.dtype)
    vc_ref[...] = proj(OFF_VC, NA_W).astype(vc_ref.dtype)
    for j in range(N_BRANCH):
        gt_ref[:, j * D_MODEL:(j + 1) * D_MODEL] = proj(OFF_GT + j * D_MODEL, D_MODEL)


def _inproj_lat(x, mod, w_in, rope_tabs, *, layer):
    n_tok = x.shape[0]
    tok = lambda width: pl.BlockSpec((TOK_TILE, width), lambda i: (i, 0))
    tiles_per_seq = DEC_SEQ // TOK_TILE
    widths = (POOL_WIDTH, DIFF_W, DIFF_W, DIFF_W, NA_W, NA_W, NA_W, N_BRANCH * D_MODEL)
    dtypes = (F32, BF16, BF16, BF16, BF16, BF16, BF16, F32)
    return pl.pallas_call(
        _inproj_lat_kernel,
        out_shape=[jax.ShapeDtypeStruct((n_tok, w), d) for w, d in zip(widths, dtypes)],
        grid=(n_tok // TOK_TILE,),
        in_specs=[
            tok(D_MODEL),
            _mod_spec(layer, tiles_per_seq),
            _layer_block((D_MODEL, D_IN), layer),
            pl.BlockSpec((TOK_TILE, LANES), lambda i: (i % tiles_per_seq, 0)),
            pl.BlockSpec((TOK_TILE, LANES), lambda i: (i % tiles_per_seq, 0)),
        ],
        out_specs=[tok(w) for w in widths],
        compiler_params=_params("parallel"),
        name="inproj_lat",
    )(x, mod, w_in, *rope_tabs)


def _diff_lambda(lq1, lk1, lq2, lk2, lam_init):
    e1 = jnp.exp(jnp.sum(lq1 * lk1, axis=-1, keepdims=True))
    e2 = jnp.exp(jnp.sum(lq2 * lk2, axis=-1, keepdims=True))
    return e1 - e2 + lam_init


def _map_masks(q):
    lane = lax.broadcasted_iota(jnp.int32, q.shape, 1)
    zero = jnp.zeros_like(q)
    return jnp.where(lane < HEAD_DIM, q, zero), jnp.where(lane >= HEAD_DIM, q, zero)


def _head_mask(shape, h):
    lane = lax.broadcasted_iota(jnp.int32, shape, 1)
    return (lane >= h * HEAD_DIM) & (lane < (h + 1) * HEAD_DIM)


def _sub_ln(o, g, lam_init):
    return o * lax.rsqrt(jnp.mean(o * o, axis=-1, keepdims=True) + RMS_EPS) * g * (1.0 - lam_init)


_LAM_SPECS = lambda layer: [_layer_block((1, HEAD_DIM), layer)] * 4 + [_layer_block((1, 2 * HEAD_DIM), layer)]


def _lam_args(lam_q1, lam_k1, lam_q2, lam_k2, subln_g):
    vec = lambda a: a.reshape(DEPTH, 1, a.shape[-1])
    return vec(lam_q1), vec(lam_k1), vec(lam_q2), vec(lam_k2), vec(subln_g)


def _ctx_attn_kernel(qb_ref, kbt_ref, vb_ref, qc_ref, kct_ref, vct_ref,
                     lq1_ref, lk1_ref, lq2_ref, lk2_ref, g_ref, yb_ref, yc_ref, *, lam_init):
    lam = _diff_lambda(lq1_ref[0], lk1_ref[0], lq2_ref[0], lk2_ref[0], lam_init)
    for h in range(DIFF_HEADS):
        sl = slice(h * 2 * HEAD_DIM, (h + 1) * 2 * HEAD_DIM)
        q0, q1 = _map_masks(qb_ref[:, sl])
        kt = kbt_ref[0, 0, sl, :].astype(BF16)
        v = vb_ref[0, 0, :, sl].astype(BF16)
        probs = []
        for q in (q0, q1):
            s = _dot(q, kt)
            e = jnp.exp(s - jnp.max(s, axis=-1, keepdims=True))
            probs.append(e * (1.0 / jnp.sum(e, axis=-1, keepdims=True)))
        a = (probs[0] - lam * probs[1]).astype(BF16)
        yb_ref[:, sl] = _sub_ln(_dot(a, v), g_ref[0], lam_init).astype(yb_ref.dtype)

    qc = qc_ref[...]
    kct = kct_ref[0, 0].astype(BF16)
    vct = vct_ref[0, 0].astype(BF16)
    out = jnp.zeros(qc.shape, F32)
    for h in range(NA_HEADS):
        mask = _head_mask(qc.shape, h)
        s = _dot(jnp.where(mask, qc, jnp.zeros_like(qc)), kct)
        e = jnp.exp(s - jnp.max(s, axis=-1, keepdims=True))
        p = (e * (1.0 / jnp.sum(e, axis=-1, keepdims=True))).astype(BF16)
        out = out + jnp.where(mask, _dot_nt(p, vct), 0.0)
    yc_ref[...] = out.astype(yc_ref.dtype)


def _ctx_attention(qb, kbt, vb, qc, kct, vct, lam_args, *, layer, lam_init):
    n_tok = qb.shape[0]
    seq = lambda width: pl.BlockSpec((SEQ, width), lambda b: (b, 0))
    kv = lambda rows, cols: pl.BlockSpec((1, 1, rows, cols), lambda b: (b, layer, 0, 0))
    return pl.pallas_call(
        functools.partial(_ctx_attn_kernel, lam_init=lam_init),
        out_shape=[jax.ShapeDtypeStruct((n_tok, DIFF_W), BF16), jax.ShapeDtypeStruct((n_tok, NA_W), BF16)],
        grid=(n_tok // SEQ,),
        in_specs=[seq(DIFF_W), kv(DIFF_W, SEQ), kv(SEQ, DIFF_W), seq(NA_W), kv(NA_W, SEQ), kv(NA_W, SEQ)]
                 + _LAM_SPECS(layer),
        out_specs=[seq(DIFF_W), seq(NA_W)],
        compiler_params=_params("parallel"),
        name="ctx_attention",
    )(qb, kbt, vb, qc, kct, vct, *lam_args)


def _lat_diff_kernel(q_ref, k_ref, v_ref, ckt_ref, cv_ref,
                     lq1_ref, lk1_ref, lq2_ref, lk2_ref, g_ref, y_ref, *, lam_init):
    lam = _diff_lambda(lq1_ref[0], lk1_ref[0], lq2_ref[0], lk2_ref[0], lam_init)
    for h in range(DIFF_HEADS):
        sl = slice(h * 2 * HEAD_DIM, (h + 1) * 2 * HEAD_DIM)
        q0, q1 = _map_masks(q_ref[0, :, sl])
        k = k_ref[0, :, sl]
        v = v_ref[0, :, sl]
        ckt = ckt_ref[0, 0, sl, :].astype(BF16)
        cv = cv_ref[0, 0, :, sl].astype(BF16)
        parts = []
        for q in (q0, q1):
            s_new = _dot_nt(q, k)
            s_old = _dot(q, ckt)
            top = jnp.maximum(jnp.max(s_new, axis=-1, keepdims=True), jnp.max(s_old, axis=-1, keepdims=True))
            e_new = jnp.exp(s_new - top)
            e_old = jnp.exp(s_old - top)
            inv = 1.0 / (jnp.sum(e_new, axis=-1, keepdims=True) + jnp.sum(e_old, axis=-1, keepdims=True))
            parts.append((e_new, e_old, inv))
        (n0, o0, r0), (n1, o1, r1) = parts
        r1 = lam * r1
        a_new = (n0 * r0 - n1 * r1).astype(BF16)
        a_old = (o0 * r0 - o1 * r1).astype(BF16)
        o = _dot(a_new, v) + _dot(a_old, cv)
        y_ref[0, :, sl] = _sub_ln(o, g_ref[0], lam_init).astype(y_ref.dtype)


def _lat_diff_attention(qb, kb, vb, cache_kt, cache_v, lam_args, *, layer, lam_init):
    full = pl.BlockSpec((1, DEC_SEQ, DIFF_W), lambda b, i: (b, 0, 0))
    past = lambda rows, cols: pl.BlockSpec((1, 1, rows, cols), lambda b, i: (b, layer, 0, 0))
    tile = pl.BlockSpec((1, MIX_TILE, DIFF_W), lambda b, i: (b, i, 0))
    return pl.pallas_call(
        functools.partial(_lat_diff_kernel, lam_init=lam_init),
        out_shape=jax.ShapeDtypeStruct((DEC_BATCH, DEC_SEQ, DIFF_W), BF16),
        grid=(DEC_BATCH, DEC_SEQ // MIX_TILE),
        in_specs=[tile, full, full, past(DIFF_W, PAST_LEN), past(PAST_LEN, DIFF_W)] + _LAM_SPECS(layer),
        out_specs=tile,
        compiler_params=_params("parallel", "parallel"),
        name="latent_diff_attention",
    )(qb, kb, vb, cache_kt, cache_v, *lam_args)


def _na_window_start(tile):
    return jnp.clip(tile * NA_Q_ROWS - NA_ROWS // 2, 0, NA_GRID_ROWS - NA_K_ROWS)


def _toeplitz(v):
    n = GRID_W
    period = 2 * n - 1
    reps = [1] * (v.ndim - 1) + [n + 1]
    flat = jnp.tile(v, reps)[..., :n * 2 * n]
    skew = flat.reshape(*v.shape[:-1], n, 2 * n)
    return skew[..., ::-1, :n]


def _na_bias_table(rpb):
    edge = GRID_W - NA_COLS
    ext = jnp.concatenate([jnp.repeat(rpb[..., :1], edge, axis=-1), rpb,
                           jnp.repeat(rpb[..., -1:], edge, axis=-1)], axis=-1)
    blocks = _toeplitz(ext)
    col = jnp.arange(GRID_W)
    col_start = jnp.clip(col - NA_COLS // 2, 0, GRID_W - NA_COLS)
    col_ok = (col[None, :] >= col_start[:, None]) & (col[None, :] < col_start[:, None] + NA_COLS)
    blocks = jnp.where(col_ok, blocks, NEG_INF)
    masked = jnp.full((NA_HEADS, GRID_W, GRID_W), NEG_INF, F32)
    tables = []
    for tile in NA_CASE_TILES:
        k0 = min(max(tile * NA_Q_ROWS - NA_ROWS // 2, 0), NA_GRID_ROWS - NA_K_ROWS)
        q_rows = []
        for qr in range(tile * NA_Q_ROWS, (tile + 1) * NA_Q_ROWS):
            row_start = min(max(qr - NA_ROWS // 2, 0), NA_GRID_ROWS - NA_ROWS)
            parts = []
            for kr in range(k0, k0 + NA_K_ROWS):
                inside = row_start <= kr < row_start + NA_ROWS
                parts.append(blocks[:, kr - qr + NA_ROWS - 1] if inside else masked)
            q_rows.append(jnp.concatenate(parts, axis=-1))
        tables.append(jnp.concatenate(q_rows, axis=1))
    return jnp.stack(tables).astype(F32)


def _lat_na_kernel(q_ref, k_ref, v_ref, ckt_ref, cvt_ref, bias_ref, y_ref):
    start = _na_window_start(pl.program_id(1))
    win = pl.ds(pl.multiple_of(start * GRID_W, NA_Q_ROWS * GRID_W), NA_K_ROWS * GRID_W)
    k = k_ref[0, win, :]
    v = v_ref[0, win, :]
    ckt = ckt_ref[0, 0].astype(BF16)
    cvt = cvt_ref[0, 0].astype(BF16)
    q = q_ref[0]
    out = jnp.zeros(q.shape, F32)
    for h in range(NA_HEADS):
        mask = _head_mask(q.shape, h)
        qh = jnp.where(mask, q, jnp.zeros_like(q))
        s_nb = _dot_nt(qh, k) + bias_ref[0, h]
        s_ctx = _dot(qh, ckt)
        top = jnp.maximum(jnp.max(s_nb, axis=-1, keepdims=True), jnp.max(s_ctx, axis=-1, keepdims=True))
        e_nb = jnp.exp(s_nb - top)
        e_ctx = jnp.exp(s_ctx - top)
        inv = 1.0 / (jnp.sum(e_nb, axis=-1, keepdims=True) + jnp.sum(e_ctx, axis=-1, keepdims=True))
        o = _dot((e_nb * inv).astype(BF16), v) + _dot_nt((e_ctx * inv).astype(BF16), cvt)
        out = out + jnp.where(mask, o, 0.0)
    y_ref[0] = out.astype(y_ref.dtype)


def _lat_na_attention(qc, kc, vc, cache_kt, cache_vt, bias, *, layer):
    q_tok = NA_Q_ROWS * GRID_W

    def bias_index(b, i):
        return (jnp.where(i == 0, 0, jnp.where(i == NA_TILES - 1, 2, 1)), 0, 0, 0)

    full = pl.BlockSpec((1, DEC_SEQ, NA_W), lambda b, i: (b, 0, 0))
    past = pl.BlockSpec((1, 1, NA_W, PAST_LEN), lambda b, i: (b, layer, 0, 0))
    tile = pl.BlockSpec((1, q_tok, NA_W), lambda b, i: (b, i, 0))
    return pl.pallas_call(
        _lat_na_kernel,
        out_shape=jax.ShapeDtypeStruct((DEC_BATCH, DEC_SEQ, NA_W), BF16),
        grid=(DEC_BATCH, NA_TILES),
        in_specs=[tile, full, full, past, past,
                  pl.BlockSpec((1, NA_HEADS, q_tok, NA_K_ROWS * GRID_W), bias_index)],
        out_specs=tile,
        compiler_params=_params("parallel", "parallel"),
        name="latent_na_attention",
    )(qc, kc, vc, cache_kt, cache_vt, bias)


def _pool(a_ref, pool_w_ref, pool_scale, seg_len):
    assert POOL_WINDOWS == tuple(2 << g for g in range(POOL_GROUPS))
    shape = (POOL_BLOCK, POOL_WIDTH)
    pos = lax.broadcasted_iota(jnp.int32, shape, 0) & (seg_len - 1)
    group = lax.broadcasted_iota(jnp.int32, shape, 1) // POOL_GROUP_W
    half = jnp.left_shift(1, group)
    count = (jnp.minimum(pos, half) + jnp.minimum(seg_len - pos, half)).astype(F32)

    def from_above(x, k):
        return jnp.where(pos >= k, pltpu.roll(x, k, 0), 0.0)

    def from_below(x, k):
        return jnp.where(pos < seg_len - k, pltpu.roll(x, POOL_BLOCK - k, 0), 0.0)

    blocks = []
    for r in range(MERGE_TILE // POOL_BLOCK):
        x = a_ref[r * POOL_BLOCK:(r + 1) * POOL_BLOCK, :]
        trail, lead = [x], [x]
        for g in range(1, POOL_GROUPS):
            k = 1 << (g - 1)
            trail.append(trail[-1] + from_above(trail[-1], k))
            lead.append(lead[-1] + from_below(lead[-1], k))
        total = None
        for g in range(POOL_GROUPS):
            window = from_above(trail[g], 1) + lead[g]
            total = window if total is None else jnp.where(group == g, window, total)
        p = total / count - x
        blocks.append(_dot(p.astype(BF16), pool_w_ref[0]) * pool_scale)
    return jnp.concatenate(blocks, axis=0)


def _merge_kernel(x_ref, m_ref, a_ref, yb_ref, yc_ref, gt_ref, pool_w_ref, pool_scale_ref,
                  wpa_ref, wpb_ref, wpc_ref, wout_ref, g_ref, b_ref, o_ref, *, seg_len):
    y_a = _pool(a_ref, pool_w_ref, pool_scale_ref[0], seg_len)
    branches = (
        _dot(y_a.astype(BF16), wpa_ref[0]),
        _dot(yb_ref[...], wpb_ref[0]),
        _dot(yc_ref[...], wpc_ref[0]),
    )
    mixed = None
    for j, br in enumerate(branches):
        term = _sigmoid(gt_ref[:, j * D_MODEL:(j + 1) * D_MODEL]) * br
        mixed = term if mixed is None else mixed + term
    h = _dot(mixed.astype(BF16), wout_ref[0])
    m = m_ref[0, 0]
    y = ALPHA * x_ref[...] + m[5:6] * h
    o_ref[...] = _layer_norm(y, g_ref[0, 1:2], b_ref[0, 1:2])


def _merge(x, mod, a, y_b, y_c, gt, pool_w, pool_scale, w_pa, w_pb, w_pc, w_out, ln_g, ln_b,
           *, layer, seg_len, tiles_per_group):
    n_tok = x.shape[0]
    tok = lambda width: pl.BlockSpec((MERGE_TILE, width), lambda i: (i, 0))
    return pl.pallas_call(
        functools.partial(_merge_kernel, seg_len=seg_len),
        out_shape=jax.ShapeDtypeStruct((n_tok, D_MODEL), F32),
        grid=(n_tok // MERGE_TILE,),
        in_specs=[
            tok(D_MODEL),
            _mod_spec(layer, tiles_per_group),
            tok(POOL_WIDTH), tok(DIFF_W), tok(NA_W), tok(N_BRANCH * D_MODEL),
            _layer_block((POOL_WIDTH, POOL_WIDTH), layer),
            _layer_block((1, POOL_WIDTH), layer),
            _layer_block((POOL_WIDTH, D_MODEL), layer),
            _layer_block((DIFF_W, D_MODEL), layer),
            _layer_block((NA_W, D_MODEL), layer),
            _layer_block((D_MODEL, D_MODEL), layer),
            _layer_block((3, D_MODEL), layer),
            _layer_block((3, D_MODEL), layer),
        ],
        out_specs=tok(D_MODEL),
        compiler_params=_params("parallel"),
        name="merge",
    )(x, mod, a, y_b, y_c, gt, pool_w, pool_scale, w_pa, w_pb, w_pc, w_out, ln_g, ln_b)


def _rope_tables():
    t = jnp.arange(DEC_SEQ)
    row = (t // GRID_W).astype(F32)
    col = (t % GRID_W).astype(F32)
    nf = HEAD_DIM // 4
    inv = ROPE_BASE ** (-jnp.arange(nf, dtype=F32) / nf)
    ang_row = row[:, None] * inv
    ang_col = col[:, None] * inv

    def half(ang):
        return (jnp.concatenate([jnp.cos(ang), jnp.cos(ang)], axis=-1),
                jnp.concatenate([-jnp.sin(ang), jnp.sin(ang)], axis=-1))

    cr, sr = half(ang_row)
    cc, sc = half(ang_col)
    cos = jnp.concatenate([cr, cc, cr, cc], axis=-1)
    sin = jnp.concatenate([sr, sc, sr, sc], axis=-1)
    return cos, sin


def kernel(x_prompt, x_sample, cache_diff_k, cache_diff_v, cache_na_k, cache_na_v, c, c_ctx, w_mod, b_mod, ln_g, ln_b, ffn1_w1, ffn1_w3, ffn1_w2, ffn2_w1, ffn2_w3, ffn2_w2, w_in, pool_w, pool_scale, w_pa, w_pb, w_pc, lam_q1, lam_k1, lam_q2, lam_k2, subln_g, na_rpb, w_out):
    cond = jnp.concatenate(
        [c_ctx[None, :], c, jnp.zeros((MOD_ROWS - 1 - DEC_BATCH, D_MODEL), F32)], axis=0)
    mod = _modulation(cond, w_mod, b_mod)
    mod = mod[:, :1 + DEC_BATCH].reshape(DEPTH, 1 + DEC_BATCH, N_MOD, D_MODEL)
    mod_ctx, mod_lat = mod[:, :1], mod[:, 1:]

    cache_dkt = cache_diff_k.transpose(0, 1, 3, 4, 5, 2).reshape(DEC_BATCH, DEPTH, DIFF_W, PAST_LEN)
    cache_dv = cache_diff_v.reshape(DEC_BATCH, DEPTH, PAST_LEN, DIFF_W)
    cache_nkt = cache_na_k.transpose(0, 1, 3, 4, 2).reshape(DEC_BATCH, DEPTH, NA_W, PAST_LEN)
    cache_nvt = cache_na_v.transpose(0, 1, 3, 4, 2).reshape(DEC_BATCH, DEPTH, NA_W, PAST_LEN)
    rope_tabs = _rope_tables()

    ffn1 = (ffn1_w1.astype(BF16), ffn1_w3.astype(BF16), ffn1_w2.astype(BF16))
    ffn2 = (ffn2_w1.astype(BF16), ffn2_w3.astype(BF16), ffn2_w2.astype(BF16))
    w_in_b = w_in.astype(BF16)
    w_in_t = jnp.concatenate([w_in[:, :, OFF_KB:OFF_KB + DIFF_W], w_in[:, :, OFF_KC:OFF_KC + 2 * NA_W]],
                             axis=-1).transpose(0, 2, 1).astype(BF16)
    group_eye = jnp.eye(POOL_GROUPS, dtype=F32)[None, :, None, :, None]
    pool_bd = (pool_w[:, :, :, None, :] * group_eye).reshape(DEPTH, POOL_WIDTH, POOL_WIDTH)
    merge_w = (pool_bd.astype(BF16), pool_scale.reshape(DEPTH, 1, POOL_WIDTH), w_pa.astype(BF16),
               w_pb.astype(BF16), w_pc.astype(BF16), w_out.astype(BF16))
    lam_args = _lam_args(lam_q1, lam_k1, lam_q2, lam_k2, subln_g)

    x_ctx = x_prompt.reshape(BATCH * SEQ, D_MODEL)
    x_lat = x_sample.reshape(DEC_BATCH * DEC_SEQ, D_MODEL)
    ctx_tiles = (BATCH * SEQ) // FFN_TILE
    lat_tiles = DEC_SEQ // FFN_TILE
    kv_ctx = None

    for l in range(DEPTH):
        lam_init = 0.8 - 0.6 * math.exp(-0.3 * l)

        x_ctx = _ffn(x_ctx, mod_ctx, *ffn1, ln_g, ln_b, layer=l, mod_idx=0, ln_idx=0, tiles_per_group=ctx_tiles)
        a, qb, kbt, vb, qc, kct, vct, gt = _inproj_ctx(x_ctx, mod_ctx, w_in_b, w_in_t, kv_ctx, layer=l)
        kv_ctx = (kbt, vb, kct, vct)
        y_b, y_c = _ctx_attention(qb, kbt, vb, qc, kct, vct, lam_args, layer=l, lam_init=lam_init)
        x_ctx = _merge(x_ctx, mod_ctx, a, y_b, y_c, gt, *merge_w, ln_g, ln_b, layer=l,
                       seg_len=SEQ, tiles_per_group=(BATCH * SEQ) // MERGE_TILE)
        x_ctx = _ffn(x_ctx, mod_ctx, *ffn2, ln_g, ln_b, layer=l, mod_idx=2, ln_idx=2, tiles_per_group=ctx_tiles)

        x_lat = _ffn(x_lat, mod_lat, *ffn1, ln_g, ln_b, layer=l, mod_idx=0, ln_idx=0, tiles_per_group=lat_tiles)
        a, qb, kb, vb, qc, kc, vc, gt = _inproj_lat(x_lat, mod_lat, w_in_b, rope_tabs, layer=l)
        seq3 = lambda z: z.reshape(DEC_BATCH, DEC_SEQ, z.shape[-1])
        y_b = _lat_diff_attention(seq3(qb), seq3(kb), seq3(vb), cache_dkt, cache_dv, lam_args,
                                  layer=l, lam_init=lam_init)
        y_c = _lat_na_attention(seq3(qc), seq3(kc), seq3(vc), cache_nkt, cache_nvt,
                                _na_bias_table(na_rpb[l]), layer=l)
        x_lat = _merge(x_lat, mod_lat, a, y_b.reshape(-1, DIFF_W), y_c.reshape(-1, NA_W), gt, *merge_w,
                       ln_g, ln_b, layer=l, seg_len=GRID_W, tiles_per_group=DEC_SEQ // MERGE_TILE)
        x_lat = _ffn(x_lat, mod_lat, *ffn2, ln_g, ln_b, layer=l, mod_idx=2, ln_idx=2, tiles_per_group=lat_tiles)

    kbt, vb, kct, vct = kv_ctx
    new_diff_k = kbt.reshape(BATCH, DEPTH, DIFF_HEADS, 2, HEAD_DIM, SEQ).transpose(0, 1, 5, 2, 3, 4)
    new_diff_v = vb.reshape(BATCH, DEPTH, SEQ, DIFF_HEADS, 2 * HEAD_DIM)
    new_na_k = kct.reshape(BATCH, DEPTH, NA_HEADS, HEAD_DIM, SEQ).transpose(0, 1, 4, 2, 3)
    new_na_v = vct.reshape(BATCH, DEPTH, NA_HEADS, HEAD_DIM, SEQ).transpose(0, 1, 4, 2, 3)
    return (x_ctx.reshape(BATCH, SEQ, D_MODEL), x_lat.reshape(DEC_BATCH, DEC_SEQ, D_MODEL),
            new_diff_k, new_diff_v, new_na_k, new_na_v)
```

```python
import functools
import math

import jax
import jax.numpy as jnp
import numpy as np
from jax import lax
from jax.experimental import pallas as pl
from jax.experimental.pallas import tpu as pltpu

D_MODEL = 1024
BATCH = 32
SEQ = 256
DEPTH = 2
DEC_BATCH = 2
DEC_SEQ = 2048
PAST_LEN = 512

GRID_W = 64
HEAD_DIM = 64
POOL_WIDTH = D_MODEL // 4
POOL_GROUPS = 4
POOL_GROUP_W = POOL_WIDTH // POOL_GROUPS
POOL_WINDOWS = (2, 4, 8, 16)
DIFF_HEADS = (D_MODEL // 2) // (2 * HEAD_DIM)
DIFF_W = DIFF_HEADS * 2 * HEAD_DIM
NA_HEADS = (D_MODEL // 4) // HEAD_DIM
NA_W = NA_HEADS * HEAD_DIM
NA_ROWS = 8
NA_COLS = 16
N_BRANCH = 3
D_IN = POOL_WIDTH + 3 * DIFF_W + 3 * NA_W + N_BRANCH * D_MODEL
D_FF = ((8 * D_MODEL // 3 + 127) // 128) * 128
N_MOD = 9
ROPE_BASE = 10000.0
LN_EPS = 1e-5
RMS_EPS = 1e-5
ALPHA = (2 * DEPTH) ** 0.25
ATTN_SCALE = HEAD_DIM ** -0.5
NEG_INF = -1e30

OFF_A = 0
OFF_QB = OFF_A + POOL_WIDTH
OFF_KB = OFF_QB + DIFF_W
OFF_VB = OFF_KB + DIFF_W
OFF_QC = OFF_VB + DIFF_W
OFF_KC = OFF_QC + NA_W
OFF_VC = OFF_KC + NA_W
OFF_GT = OFF_VC + NA_W
KT_W = DIFF_W + 2 * NA_W

LANES = 128
MXU_DIM = 256
VMEM_LIMIT = 56 * 1024 * 1024

FF_CHUNK = MXU_DIM
N_FF_CHUNKS = D_FF // FF_CHUNK
TOK_TILE = 512
FFN_TILE = 1024
MIX_TILE = 256
MERGE_TILE = 512
POOL_BLOCK = 256
MOD_ROWS = 16
MOD_COLS = 1024
NA_Q_ROWS = 4
NA_K_ROWS = NA_Q_ROWS + NA_ROWS
NA_GRID_ROWS = DEC_SEQ // GRID_W
NA_TILES = NA_GRID_ROWS // NA_Q_ROWS
NA_CASE_TILES = (0, 1, NA_TILES - 1)
SEQ_PER_TILE = TOK_TILE // SEQ

BF16 = jnp.bfloat16
F32 = jnp.float32


def _dot(a, b):
    return jnp.dot(a, b, preferred_element_type=F32)


def _dot_nt(a, b):
    return lax.dot_general(a, b, (((1,), (1,)), ((), ())), preferred_element_type=F32)


def _sigmoid(x):
    return 0.5 * jnp.tanh(0.5 * x) + 0.5


def _layer_norm(y, g, b):
    mu = jnp.mean(y, axis=-1, keepdims=True)
    d = y - mu
    var = jnp.mean(d * d, axis=-1, keepdims=True)
    return d * lax.rsqrt(var + LN_EPS) * g + b


def _resident(shape, index=None):
    index = (0,) * len(shape) if index is None else index
    return pl.BlockSpec(shape, lambda *_: index, pipeline_mode=pl.Buffered(1))


def _layer_block(shape, layer):
    return _resident((1,) + shape, (layer,) + (0,) * len(shape))


def _params(*sem):
    return pltpu.CompilerParams(dimension_semantics=sem, vmem_limit_bytes=VMEM_LIMIT)


def _mod_kernel(cond_ref, w_ref, b_ref, o_ref):
    c = cond_ref[...]
    x = c * _sigmoid(c)
    w = w_ref[0]
    x_hi = x.astype(BF16)
    x_lo = (x - x_hi.astype(F32)).astype(BF16)
    w_hi = w.astype(BF16)
    w_lo = (w - w_hi.astype(F32)).astype(BF16)
    o_ref[0] = _dot(x_hi, w_hi) + _dot(x_lo, w_hi) + _dot(x_hi, w_lo) + b_ref[0]


def _modulation(cond, w_mod, b_mod):
    n_cols = N_MOD * D_MODEL
    return pl.pallas_call(
        _mod_kernel,
        out_shape=jax.ShapeDtypeStruct((DEPTH, MOD_ROWS, n_cols), F32),
        grid=(DEPTH, n_cols // MOD_COLS),
        in_specs=[
            pl.BlockSpec((MOD_ROWS, D_MODEL), lambda l, j: (0, 0)),
            pl.BlockSpec((1, D_MODEL, MOD_COLS), lambda l, j: (l, 0, j)),
            pl.BlockSpec((1, 1, MOD_COLS), lambda l, j: (l, 0, j)),
        ],
        out_specs=pl.BlockSpec((1, MOD_ROWS, MOD_COLS), lambda l, j: (l, 0, j)),
        compiler_params=_params("parallel", "parallel"),
        name="modulation",
    )(cond, w_mod, b_mod.reshape(DEPTH, 1, n_cols))


def _ffn_kernel(x_ref, m_ref, w1_ref, w3_ref, w2_ref, g_ref, b_ref, o_ref, xm_ref, acc_ref, *, mod_idx, ln_idx):
    x = x_ref[...]
    m = m_ref[0, 0]
    shift = m[3 * mod_idx:3 * mod_idx + 1]
    scale = m[3 * mod_idx + 1:3 * mod_idx + 2]
    gate = m[3 * mod_idx + 2:3 * mod_idx + 3]
    xm_ref[...] = (x * (1.0 + scale) + shift).astype(BF16)

    def chunk_out(c):
        cols = pl.ds(pl.multiple_of(c * FF_CHUNK, FF_CHUNK), FF_CHUNK)
        xm = xm_ref[...]
        h1 = _dot(xm, w1_ref[0, :, cols])
        h3 = _dot(xm, w3_ref[0, :, cols])
        act = (h1 * _sigmoid(h1)) * h3
        return _dot(act.astype(BF16), w2_ref[0, cols, :])

    acc_ref[...] = chunk_out(0)

    def chunk(c, carry):
        acc_ref[...] += chunk_out(c)
        return carry

    lax.fori_loop(1, N_FF_CHUNKS, chunk, 0, unroll=True)
    y = ALPHA * x + (0.5 * gate) * acc_ref[...]
    o_ref[...] = _layer_norm(y, g_ref[0, ln_idx:ln_idx + 1], b_ref[0, ln_idx:ln_idx + 1])


def _mod_spec(layer, tiles_per_group):
    return pl.BlockSpec((1, 1, N_MOD, D_MODEL), lambda i: (layer, i // tiles_per_group, 0, 0))


def _ffn(x, mod, w1, w3, w2, ln_g, ln_b, *, layer, mod_idx, ln_idx, tiles_per_group):
    n_tok = x.shape[0]
    return pl.pallas_call(
        functools.partial(_ffn_kernel, mod_idx=mod_idx, ln_idx=ln_idx),
        out_shape=jax.ShapeDtypeStruct((n_tok, D_MODEL), F32),
        grid=(n_tok // FFN_TILE,),
        in_specs=[
            pl.BlockSpec((FFN_TILE, D_MODEL), lambda i: (i, 0)),
            _mod_spec(layer, tiles_per_group),
            _layer_block((D_MODEL, D_FF), layer),
            _layer_block((D_MODEL, D_FF), layer),
            _layer_block((D_FF, D_MODEL), layer),
            _layer_block((3, D_MODEL), layer),
            _layer_block((3, D_MODEL), layer),
        ],
        out_specs=pl.BlockSpec((FFN_TILE, D_MODEL), lambda i: (i, 0)),
        scratch_shapes=[pltpu.VMEM((FFN_TILE, D_MODEL), BF16), pltpu.VMEM((FFN_TILE, D_MODEL), F32)],
        compiler_params=_params("parallel"),
        name="ffn",
    )(x, mod, w1, w3, w2, ln_g, ln_b)


def _rope(x, cos, sin_signed):
    lane = lax.broadcasted_iota(jnp.int32, x.shape, 1)
    first = (lane % (HEAD_DIM // 2)) < (HEAD_DIM // 4)
    partner = jnp.where(first, pltpu.roll(x, LANES - HEAD_DIM // 4, 1), pltpu.roll(x, HEAD_DIM // 4, 1))
    return x * cos + partner * sin_signed


def _modulated(x_ref, m_ref):
    m = m_ref[0, 0]
    return (x_ref[...] * (1.0 + m[4:5]) + m[3:4]).astype(BF16)


def _inproj_ctx_kernel(*refs, slab):
    x_ref, m_ref, w_ref, wt_ref = refs[:4]
    a_ref, qb_ref, kbt_ref, vb_ref, qc_ref, kct_ref, vct_ref, gt_ref = refs[-8:]
    xm = _modulated(x_ref, m_ref)

    def proj(off, width):
        return _dot(xm, w_ref[0, :, off:off + width])

    a_ref[...] = proj(OFF_A, POOL_WIDTH)
    qb_ref[...] = (proj(OFF_QB, DIFF_W) * ATTN_SCALE).astype(qb_ref.dtype)
    qc_ref[...] = (proj(OFF_QC, NA_W) * ATTN_SCALE).astype(qc_ref.dtype)
    v = proj(OFF_VB, DIFF_W)
    for j in range(N_BRANCH):
        gt_ref[:, j * D_MODEL:(j + 1) * D_MODEL] = proj(OFF_GT + j * D_MODEL, D_MODEL)
    zt = _dot_nt(wt_ref[0], xm)
    for s in range(SEQ_PER_TILE):
        tok = slice(s * SEQ, (s + 1) * SEQ)
        for h in range(DIFF_HEADS):
            vb_ref[s, slab, pl.ds(h, SEQ, stride=DIFF_HEADS), :] = v[tok, h * 2 * HEAD_DIM:(h + 1) * 2 * HEAD_DIM]
        kbt_ref[s, slab] = zt[:DIFF_W, tok]
        kct_ref[s, slab] = zt[DIFF_W:DIFF_W + NA_W, tok]
        vct_ref[s, slab] = zt[DIFF_W + NA_W:, tok]
        for other in range(vb_ref.shape[1]):
            if other != slab:
                for ref in (vb_ref, kbt_ref, kct_ref, vct_ref):
                    ref[s, other] = jnp.zeros(ref.shape[2:], ref.dtype)


def _inproj_ctx(x, mod, w_in, w_in_t, kv_prev, *, layer):
    n_tok = x.shape[0]
    tok = lambda width: pl.BlockSpec((TOK_TILE, width), lambda i: (i, 0))
    if kv_prev is None:
        slab = layer
        kv = lambda rows, cols: pl.BlockSpec((SEQ_PER_TILE, DEPTH, rows, cols), lambda i: (i, 0, 0, 0))
    else:
        slab = 0
        kv = lambda rows, cols: pl.BlockSpec((SEQ_PER_TILE, 1, rows, cols), lambda i: (i, layer, 0, 0))
    kv_shape = lambda rows, cols: jax.ShapeDtypeStruct((BATCH, DEPTH, rows, cols), F32)
    in_specs = [
        tok(D_MODEL),
        _mod_spec(layer, n_tok // TOK_TILE),
        _layer_block((D_MODEL, D_IN), layer),
        _layer_block((KT_W, D_MODEL), layer),
    ]
    args = [x, mod, w_in, w_in_t]
    aliases = {}
    if kv_prev is not None:
        for arr, out_idx in zip(kv_prev, (2, 3, 5, 6)):
            aliases[len(args)] = out_idx
            args.append(arr)
            in_specs.append(pl.BlockSpec(memory_space=pl.ANY))
    return pl.pallas_call(
        functools.partial(_inproj_ctx_kernel, slab=slab),
        out_shape=[
            jax.ShapeDtypeStruct((n_tok, POOL_WIDTH), F32),
            jax.ShapeDtypeStruct((n_tok, DIFF_W), BF16),
            kv_shape(DIFF_W, SEQ),
            kv_shape(SEQ * DIFF_HEADS, 2 * HEAD_DIM),
            jax.ShapeDtypeStruct((n_tok, NA_W), BF16),
            kv_shape(NA_W, SEQ),
            kv_shape(NA_W, SEQ),
            jax.ShapeDtypeStruct((n_tok, N_BRANCH * D_MODEL), F32),
        ],
        grid=(n_tok // TOK_TILE,),
        in_specs=in_specs,
        out_specs=[tok(POOL_WIDTH), tok(DIFF_W), kv(DIFF_W, SEQ), kv(SEQ * DIFF_HEADS, 2 * HEAD_DIM),
                   tok(NA_W), kv(NA_W, SEQ), kv(NA_W, SEQ), tok(N_BRANCH * D_MODEL)],
        input_output_aliases=aliases,
        compiler_params=_params("parallel"),
        name="inproj_ctx",
    )(*args)


def _inproj_lat_kernel(x_ref, m_ref, w_ref, cos_ref, sin_ref,
                       a_ref, qb_ref, kb_ref, vb_ref, qc_ref, kc_ref, vc_ref, gt_ref):
    xm = _modulated(x_ref, m_ref)

    def proj(off, width):
        return _dot(xm, w_ref[0, :, off:off + width])

    a_ref[...] = proj(OFF_A, POOL_WIDTH)
    for h in range(DIFF_HEADS):
        lo = h * 2 * HEAD_DIM
        q = _rope(proj(OFF_QB + lo, 2 * HEAD_DIM), cos_ref[...], sin_ref[...])
        k = _rope(proj(OFF_KB + lo, 2 * HEAD_DIM), cos_ref[...], sin_ref[...])
        qb_ref[:, lo:lo + 2 * HEAD_DIM] = (q * ATTN_SCALE).astype(qb_ref.dtype)
        kb_ref[:, lo:lo + 2 * HEAD_DIM] = k.astype(kb_ref.dtype)
    vb_ref[...] = proj(OFF_VB, DIFF_W).astype(vb_ref.dtype)
    qc_ref[...] = (proj(OFF_QC, NA_W) * ATTN_SCALE).astype(qc_ref.dtype)
    kc_ref[...] = proj(OFF_KC, NA_W).astype(kc_ref.dtype)
    vc_ref[...] = proj(OFF_VC, NA_W).astype(vc_ref.dtype)
    for j in range(N_BRANCH):
        gt_ref[:, j * D_MODEL:(j + 1) * D_MODEL] = proj(OFF_GT + j * D_MODEL, D_MODEL)


def _inproj_lat(x, mod, w_in, rope_tabs, *, layer):
    n_tok = x.shape[0]
    tok = lambda width: pl.BlockSpec((TOK_TILE, width), lambda i: (i, 0))
    tiles_per_seq = DEC_SEQ // TOK_TILE
    widths = (POOL_WIDTH, DIFF_W, DIFF_W, DIFF_W, NA_W, NA_W, NA_W, N_BRANCH * D_MODEL)
    dtypes = (F32, BF16, BF16, BF16, BF16, BF16, BF16, F32)
    return pl.pallas_call(
        _inproj_lat_kernel,
        out_shape=[jax.ShapeDtypeStruct((n_tok, w), d) for w, d in zip(widths, dtypes)],
        grid=(n_tok // TOK_TILE,),
        in_specs=[
            tok(D_MODEL),
            _mod_spec(layer, tiles_per_seq),
            _layer_block((D_MODEL, D_IN), layer),
            pl.BlockSpec((TOK_TILE, LANES), lambda i: (i % tiles_per_seq, 0)),
            pl.BlockSpec((TOK_TILE, LANES), lambda i: (i % tiles_per_seq, 0)),
        ],
        out_specs=[tok(w) for w in widths],
        compiler_params=_params("parallel"),
        name="inproj_lat",
    )(x, mod, w_in, *rope_tabs)


def _diff_lambda(lq1, lk1, lq2, lk2, lam_init):
    e1 = jnp.exp(jnp.sum(lq1 * lk1, axis=-1, keepdims=True))
    e2 = jnp.exp(jnp.sum(lq2 * lk2, axis=-1, keepdims=True))
    return e1 - e2 + lam_init


def _map_masks(q):
    lane = lax.broadcasted_iota(jnp.int32, q.shape, 1)
    zero = jnp.zeros_like(q)
    return jnp.where(lane < HEAD_DIM, q, zero), jnp.where(lane >= HEAD_DIM, q, zero)


def _head_mask(shape, h):
    lane = lax.broadcasted_iota(jnp.int32, shape, 1)
    return (lane >= h * HEAD_DIM) & (lane < (h + 1) * HEAD_DIM)


def _sub_ln(o, g, lam_init):
    return o * lax.rsqrt(jnp.mean(o * o, axis=-1, keepdims=True) + RMS_EPS) * g * (1.0 - lam_init)


_LAM_SPECS = lambda layer: [_layer_block((1, HEAD_DIM), layer)] * 4 + [_layer_block((1, 2 * HEAD_DIM), layer)]


def _lam_args(lam_q1, lam_k1, lam_q2, lam_k2, subln_g):
    vec = lambda a: a.reshape(DEPTH, 1, a.shape[-1])
    return vec(lam_q1), vec(lam_k1), vec(lam_q2), vec(lam_k2), vec(subln_g)


def _ctx_attn_kernel(qb_ref, kbt_ref, vb_ref, qc_ref, kct_ref, vct_ref,
                     lq1_ref, lk1_ref, lq2_ref, lk2_ref, g_ref, yb_ref, yc_ref, *, lam_init):
    lam = _diff_lambda(lq1_ref[0], lk1_ref[0], lq2_ref[0], lk2_ref[0], lam_init)
    for h in range(DIFF_HEADS):
        sl = slice(h * 2 * HEAD_DIM, (h + 1) * 2 * HEAD_DIM)
        q0, q1 = _map_masks(qb_ref[:, sl])
        kt = kbt_ref[0, 0, sl, :].astype(BF16)
        v = vb_ref[0, 0, pl.ds(h, SEQ, stride=DIFF_HEADS), :].astype(BF16)
        probs = []
        for q in (q0, q1):
            s = _dot(q, kt)
            e = jnp.exp(s - jnp.max(s, axis=-1, keepdims=True))
            probs.append(e * (1.0 / jnp.sum(e, axis=-1, keepdims=True)))
        a = (probs[0] - lam * probs[1]).astype(BF16)
        yb_ref[:, sl] = _sub_ln(_dot(a, v), g_ref[0], lam_init).astype(yb_ref.dtype)

    qc = qc_ref[...]
    kct = kct_ref[0, 0].astype(BF16)
    vct = vct_ref[0, 0].astype(BF16)
    out = jnp.zeros(qc.shape, F32)
    for h in range(NA_HEADS):
        mask = _head_mask(qc.shape, h)
        s = _dot(jnp.where(mask, qc, jnp.zeros_like(qc)), kct)
        e = jnp.exp(s - jnp.max(s, axis=-1, keepdims=True))
        p = (e * (1.0 / jnp.sum(e, axis=-1, keepdims=True))).astype(BF16)
        out = out + jnp.where(mask, _dot_nt(p, vct), 0.0)
    yc_ref[...] = out.astype(yc_ref.dtype)


def _ctx_attention(qb, kbt, vb, qc, kct, vct, lam_args, *, layer, lam_init):
    n_tok = qb.shape[0]
    seq = lambda width: pl.BlockSpec((SEQ, width), lambda b: (b, 0))
    kv = lambda rows, cols: pl.BlockSpec((1, 1, rows, cols), lambda b: (b, layer, 0, 0))
    return pl.pallas_call(
        functools.partial(_ctx_attn_kernel, lam_init=lam_init),
        out_shape=[jax.ShapeDtypeStruct((n_tok, DIFF_W), BF16), jax.ShapeDtypeStruct((n_tok, NA_W), BF16)],
        grid=(n_tok // SEQ,),
        in_specs=[seq(DIFF_W), kv(DIFF_W, SEQ), kv(SEQ * DIFF_HEADS, 2 * HEAD_DIM), seq(NA_W), kv(NA_W, SEQ),
                  kv(NA_W, SEQ)]
                 + _LAM_SPECS(layer),
        out_specs=[seq(DIFF_W), seq(NA_W)],
        compiler_params=_params("parallel"),
        name="ctx_attention",
    )(qb, kbt, vb, qc, kct, vct, *lam_args)


def _lat_diff_kernel(q_ref, k_ref, v_ref, ckt_ref, cv_ref,
                     lq1_ref, lk1_ref, lq2_ref, lk2_ref, g_ref, y_ref, *, lam_init):
    lam = _diff_lambda(lq1_ref[0], lk1_ref[0], lq2_ref[0], lk2_ref[0], lam_init)
    for h in range(DIFF_HEADS):
        sl = slice(h * 2 * HEAD_DIM, (h + 1) * 2 * HEAD_DIM)
        q0, q1 = _map_masks(q_ref[0, :, sl])
        k = k_ref[0, :, sl]
        v = v_ref[0, :, sl]
        ckt = ckt_ref[0, 0, sl, :].astype(BF16)
        cv = cv_ref[0, 0, :, sl].astype(BF16)
        parts = []
        for q in (q0, q1):
            s_new = _dot_nt(q, k)
            s_old = _dot(q, ckt)
            top = jnp.maximum(jnp.max(s_new, axis=-1, keepdims=True), jnp.max(s_old, axis=-1, keepdims=True))
            e_new = jnp.exp(s_new - top)
            e_old = jnp.exp(s_old - top)
            inv = 1.0 / (jnp.sum(e_new, axis=-1, keepdims=True) + jnp.sum(e_old, axis=-1, keepdims=True))
            parts.append((e_new, e_old, inv))
        (n0, o0, r0), (n1, o1, r1) = parts
        r1 = lam * r1
        a_new = (n0 * r0 - n1 * r1).astype(BF16)
        a_old = (o0 * r0 - o1 * r1).astype(BF16)
        o = _dot(a_new, v) + _dot(a_old, cv)
        y_ref[0, :, sl] = _sub_ln(o, g_ref[0], lam_init).astype(y_ref.dtype)


def _lat_diff_attention(qb, kb, vb, cache_kt, cache_v, lam_args, *, layer, lam_init):
    full = pl.BlockSpec((1, DEC_SEQ, DIFF_W), lambda b, i: (b, 0, 0))
    past = lambda rows, cols: pl.BlockSpec((1, 1, rows, cols), lambda b, i: (b, layer, 0, 0))
    tile = pl.BlockSpec((1, MIX_TILE, DIFF_W), lambda b, i: (b, i, 0))
    return pl.pallas_call(
        functools.partial(_lat_diff_kernel, lam_init=lam_init),
        out_shape=jax.ShapeDtypeStruct((DEC_BATCH, DEC_SEQ, DIFF_W), BF16),
        grid=(DEC_BATCH, DEC_SEQ // MIX_TILE),
        in_specs=[tile, full, full, past(DIFF_W, PAST_LEN), past(PAST_LEN, DIFF_W)] + _LAM_SPECS(layer),
        out_specs=tile,
        compiler_params=_params("parallel", "parallel"),
        name="latent_diff_attention",
    )(qb, kb, vb, cache_kt, cache_v, *lam_args)


def _na_window_start(tile):
    return jnp.clip(tile * NA_Q_ROWS - NA_ROWS // 2, 0, NA_GRID_ROWS - NA_K_ROWS)


def _toeplitz(v):
    n = GRID_W
    period = 2 * n - 1
    reps = [1] * (v.ndim - 1) + [n + 1]
    flat = jnp.tile(v, reps)[..., :n * 2 * n]
    skew = flat.reshape(*v.shape[:-1], n, 2 * n)
    return skew[..., ::-1, :n]


def _na_bias_table(rpb):
    edge = GRID_W - NA_COLS
    ext = jnp.concatenate([jnp.repeat(rpb[..., :1], edge, axis=-1), rpb,
                           jnp.repeat(rpb[..., -1:], edge, axis=-1)], axis=-1)
    blocks = _toeplitz(ext)
    col = jnp.arange(GRID_W)
    col_start = jnp.clip(col - NA_COLS // 2, 0, GRID_W - NA_COLS)
    col_ok = (col[None, :] >= col_start[:, None]) & (col[None, :] < col_start[:, None] + NA_COLS)
    blocks = jnp.where(col_ok, blocks, NEG_INF)
    masked = jnp.full((NA_HEADS, GRID_W, GRID_W), NEG_INF, F32)
    tables = []
    for tile in NA_CASE_TILES:
        k0 = min(max(tile * NA_Q_ROWS - NA_ROWS // 2, 0), NA_GRID_ROWS - NA_K_ROWS)
        q_rows = []
        for qr in range(tile * NA_Q_ROWS, (tile + 1) * NA_Q_ROWS):
            row_start = min(max(qr - NA_ROWS // 2, 0), NA_GRID_ROWS - NA_ROWS)
            parts = []
            for kr in range(k0, k0 + NA_K_ROWS):
                inside = row_start <= kr < row_start + NA_ROWS
                parts.append(blocks[:, kr - qr + NA_ROWS - 1] if inside else masked)
            q_rows.append(jnp.concatenate(parts, axis=-1))
        tables.append(jnp.concatenate(q_rows, axis=1))
    return jnp.stack(tables).astype(F32)


def _lat_na_kernel(q_ref, k_ref, v_ref, ckt_ref, cvt_ref, bias_ref, y_ref):
    start = _na_window_start(pl.program_id(1))
    win = pl.ds(pl.multiple_of(start * GRID_W, NA_Q_ROWS * GRID_W), NA_K_ROWS * GRID_W)
    k = k_ref[0, win, :]
    v = v_ref[0, win, :]
    ckt = ckt_ref[0, 0].astype(BF16)
    cvt = cvt_ref[0, 0].astype(BF16)
    q = q_ref[0]
    out = jnp.zeros(q.shape, F32)
    for h in range(NA_HEADS):
        mask = _head_mask(q.shape, h)
        qh = jnp.where(mask, q, jnp.zeros_like(q))
        s_nb = _dot_nt(qh, k) + bias_ref[0, h]
        s_ctx = _dot(qh, ckt)
        top = jnp.maximum(jnp.max(s_nb, axis=-1, keepdims=True), jnp.max(s_ctx, axis=-1, keepdims=True))
        e_nb = jnp.exp(s_nb - top)
        e_ctx = jnp.exp(s_ctx - top)
        inv = 1.0 / (jnp.sum(e_nb, axis=-1, keepdims=True) + jnp.sum(e_ctx, axis=-1, keepdims=True))
        o = _dot((e_nb * inv).astype(BF16), v) + _dot_nt((e_ctx * inv).astype(BF16), cvt)
        out = out + jnp.where(mask, o, 0.0)
    y_ref[0] = out.astype(y_ref.dtype)


def _lat_na_attention(qc, kc, vc, cache_kt, cache_vt, bias, *, layer):
    q_tok = NA_Q_ROWS * GRID_W

    def bias_index(b, i):
        return (jnp.where(i == 0, 0, jnp.where(i == NA_TILES - 1, 2, 1)), 0, 0, 0)

    full = pl.BlockSpec((1, DEC_SEQ, NA_W), lambda b, i: (b, 0, 0))
    past = pl.BlockSpec((1, 1, NA_W, PAST_LEN), lambda b, i: (b, layer, 0, 0))
    tile = pl.BlockSpec((1, q_tok, NA_W), lambda b, i: (b, i, 0))
    return pl.pallas_call(
        _lat_na_kernel,
        out_shape=jax.ShapeDtypeStruct((DEC_BATCH, DEC_SEQ, NA_W), BF16),
        grid=(DEC_BATCH, NA_TILES),
        in_specs=[tile, full, full, past, past,
                  pl.BlockSpec((1, NA_HEADS, q_tok, NA_K_ROWS * GRID_W), bias_index)],
        out_specs=tile,
        compiler_params=_params("parallel", "parallel"),
        name="latent_na_attention",
    )(qc, kc, vc, cache_kt, cache_vt, bias)


def _pool(a_ref, pool_w_ref, pool_scale, seg_len):
    assert POOL_WINDOWS == tuple(2 << g for g in range(POOL_GROUPS))
    shape = (POOL_BLOCK, POOL_WIDTH)
    pos = lax.broadcasted_iota(jnp.int32, shape, 0) & (seg_len - 1)
    group = lax.broadcasted_iota(jnp.int32, shape, 1) // POOL_GROUP_W
    half = jnp.left_shift(1, group)
    count = (jnp.minimum(pos, half) + jnp.minimum(seg_len - pos, half)).astype(F32)

    def from_above(x, k):
        return jnp.where(pos >= k, pltpu.roll(x, k, 0), 0.0)

    def from_below(x, k):
        return jnp.where(pos < seg_len - k, pltpu.roll(x, POOL_BLOCK - k, 0), 0.0)

    blocks = []
    for r in range(MERGE_TILE // POOL_BLOCK):
        x = a_ref[r * POOL_BLOCK:(r + 1) * POOL_BLOCK, :]
        trail, lead = [x], [x]
        for g in range(1, POOL_GROUPS):
            k = 1 << (g - 1)
            trail.append(trail[-1] + from_above(trail[-1], k))
            lead.append(lead[-1] + from_below(lead[-1], k))
        total = None
        for g in range(POOL_GROUPS):
            window = from_above(trail[g], 1) + lead[g]
            total = window if total is None else jnp.where(group == g, window, total)
        p = total / count - x
        blocks.append(_dot(p.astype(BF16), pool_w_ref[0]) * pool_scale)
    return jnp.concatenate(blocks, axis=0)


def _merge_kernel(x_ref, m_ref, a_ref, yb_ref, yc_ref, gt_ref, pool_w_ref, pool_scale_ref,
                  wpa_ref, wpb_ref, wpc_ref, wout_ref, g_ref, b_ref, o_ref, *, seg_len):
    y_a = _pool(a_ref, pool_w_ref, pool_scale_ref[0], seg_len)
    branches = (
        _dot(y_a.astype(BF16), wpa_ref[0]),
        _dot(yb_ref[...], wpb_ref[0]),
        _dot(yc_ref[...], wpc_ref[0]),
    )
    mixed = None
    for j, br in enumerate(branches):
        term = _sigmoid(gt_ref[:, j * D_MODEL:(j + 1) * D_MODEL]) * br
        mixed = term if mixed is None else mixed + term
    h = _dot(mixed.astype(BF16), wout_ref[0])
    m = m_ref[0, 0]
    y = ALPHA * x_ref[...] + m[5:6] * h
    o_ref[...] = _layer_norm(y, g_ref[0, 1:2], b_ref[0, 1:2])


def _merge(x, mod, a, y_b, y_c, gt, pool_w, pool_scale, w_pa, w_pb, w_pc, w_out, ln_g, ln_b,
           *, layer, seg_len, tiles_per_group):
    n_tok = x.shape[0]
    tok = lambda width: pl.BlockSpec((MERGE_TILE, width), lambda i: (i, 0))
    return pl.pallas_call(
        functools.partial(_merge_kernel, seg_len=seg_len),
        out_shape=jax.ShapeDtypeStruct((n_tok, D_MODEL), F32),
        grid=(n_tok // MERGE_TILE,),
        in_specs=[
            tok(D_MODEL),
            _mod_spec(layer, tiles_per_group),
            tok(POOL_WIDTH), tok(DIFF_W), tok(NA_W), tok(N_BRANCH * D_MODEL),
            _layer_block((POOL_WIDTH, POOL_WIDTH), layer),
            _layer_block((1, POOL_WIDTH), layer),
            _layer_block((POOL_WIDTH, D_MODEL), layer),
            _layer_block((DIFF_W, D_MODEL), layer),
            _layer_block((NA_W, D_MODEL), layer),
            _layer_block((D_MODEL, D_MODEL), layer),
            _layer_block((3, D_MODEL), layer),
            _layer_block((3, D_MODEL), layer),
        ],
        out_specs=tok(D_MODEL),
        compiler_params=_params("parallel"),
        name="merge",
    )(x, mod, a, y_b, y_c, gt, pool_w, pool_scale, w_pa, w_pb, w_pc, w_out, ln_g, ln_b)


def _rope_tables():
    t = np.arange(DEC_SEQ)
    nf = HEAD_DIM // 4
    inv = ROPE_BASE ** (-np.arange(nf, dtype=np.float64) / nf)
    ang_row = (t // GRID_W)[:, None] * inv
    ang_col = (t % GRID_W)[:, None] * inv

    def half(ang):
        return (np.concatenate([np.cos(ang), np.cos(ang)], axis=-1),
                np.concatenate([-np.sin(ang), np.sin(ang)], axis=-1))

    cr, sr = half(ang_row)
    cc, sc = half(ang_col)
    cos = np.concatenate([cr, cc, cr, cc], axis=-1)
    sin = np.concatenate([sr, sc, sr, sc], axis=-1)
    return jnp.asarray(cos, F32), jnp.asarray(sin, F32)


def kernel(x_prompt, x_sample, cache_diff_k, cache_diff_v, cache_na_k, cache_na_v, c, c_ctx, w_mod, b_mod, ln_g, ln_b, ffn1_w1, ffn1_w3, ffn1_w2, ffn2_w1, ffn2_w3, ffn2_w2, w_in, pool_w, pool_scale, w_pa, w_pb, w_pc, lam_q1, lam_k1, lam_q2, lam_k2, subln_g, na_rpb, w_out):
    cond = jnp.concatenate(
        [c_ctx[None, :], c, jnp.zeros((MOD_ROWS - 1 - DEC_BATCH, D_MODEL), F32)], axis=0)
    mod = _modulation(cond, w_mod, b_mod)
    mod = mod[:, :1 + DEC_BATCH].reshape(DEPTH, 1 + DEC_BATCH, N_MOD, D_MODEL)
    mod_ctx, mod_lat = mod[:, :1], mod[:, 1:]

    cache_dkt = cache_diff_k.transpose(0, 1, 3, 4, 5, 2).reshape(DEC_BATCH, DEPTH, DIFF_W, PAST_LEN)
    cache_dv = cache_diff_v.reshape(DEC_BATCH, DEPTH, PAST_LEN, DIFF_W)
    cache_nkt = cache_na_k.transpose(0, 1, 3, 4, 2).reshape(DEC_BATCH, DEPTH, NA_W, PAST_LEN)
    cache_nvt = cache_na_v.transpose(0, 1, 3, 4, 2).reshape(DEC_BATCH, DEPTH, NA_W, PAST_LEN)
    rope_tabs = _rope_tables()

    ffn1 = (ffn1_w1.astype(BF16), ffn1_w3.astype(BF16), ffn1_w2.astype(BF16))
    ffn2 = (ffn2_w1.astype(BF16), ffn2_w3.astype(BF16), ffn2_w2.astype(BF16))
    w_in_b = w_in.astype(BF16)
    w_in_t = jnp.concatenate([w_in[:, :, OFF_KB:OFF_KB + DIFF_W], w_in[:, :, OFF_KC:OFF_KC + 2 * NA_W]],
                             axis=-1).transpose(0, 2, 1).astype(BF16)
    group_eye = jnp.eye(POOL_GROUPS, dtype=F32)[None, :, None, :, None]
    pool_bd = (pool_w[:, :, :, None, :] * group_eye).reshape(DEPTH, POOL_WIDTH, POOL_WIDTH)
    merge_w = (pool_bd.astype(BF16), pool_scale.reshape(DEPTH, 1, POOL_WIDTH), w_pa.astype(BF16),
               w_pb.astype(BF16), w_pc.astype(BF16), w_out.astype(BF16))
    lam_args = _lam_args(lam_q1, lam_k1, lam_q2, lam_k2, subln_g)

    x_ctx = x_prompt.reshape(BATCH * SEQ, D_MODEL)
    x_lat = x_sample.reshape(DEC_BATCH * DEC_SEQ, D_MODEL)
    ctx_tiles = (BATCH * SEQ) // FFN_TILE
    lat_tiles = DEC_SEQ // FFN_TILE
    kv_ctx = None

    for l in range(DEPTH):
        lam_init = 0.8 - 0.6 * math.exp(-0.3 * l)

        x_ctx = _ffn(x_ctx, mod_ctx, *ffn1, ln_g, ln_b, layer=l, mod_idx=0, ln_idx=0, tiles_per_group=ctx_tiles)
        a, qb, kbt, vb, qc, kct, vct, gt = _inproj_ctx(x_ctx, mod_ctx, w_in_b, w_in_t, kv_ctx, layer=l)
        kv_ctx = (kbt, vb, kct, vct)
        y_b, y_c = _ctx_attention(qb, kbt, vb, qc, kct, vct, lam_args, layer=l, lam_init=lam_init)
        x_ctx = _merge(x_ctx, mod_ctx, a, y_b, y_c, gt, *merge_w, ln_g, ln_b, layer=l,
                       seg_len=SEQ, tiles_per_group=(BATCH * SEQ) // MERGE_TILE)
        x_ctx = _ffn(x_ctx, mod_ctx, *ffn2, ln_g, ln_b, layer=l, mod_idx=2, ln_idx=2, tiles_per_group=ctx_tiles)

        x_lat = _ffn(x_lat, mod_lat, *ffn1, ln_g, ln_b, layer=l, mod_idx=0, ln_idx=0, tiles_per_group=lat_tiles)
        a, qb, kb, vb, qc, kc, vc, gt = _inproj_lat(x_lat, mod_lat, w_in_b, rope_tabs, layer=l)
        seq3 = lambda z: z.reshape(DEC_BATCH, DEC_SEQ, z.shape[-1])
        y_b = _lat_diff_attention(seq3(qb), seq3(kb), seq3(vb), cache_dkt, cache_dv, lam_args,
                                  layer=l, lam_init=lam_init)
        y_c = _lat_na_attention(seq3(qc), seq3(kc), seq3(vc), cache_nkt, cache_nvt,
                                _na_bias_table(na_rpb[l]), layer=l)
        x_lat = _merge(x_lat, mod_lat, a, y_b.reshape(-1, DIFF_W), y_c.reshape(-1, NA_W), gt, *merge_w,
                       ln_g, ln_b, layer=l, seg_len=GRID_W, tiles_per_group=DEC_SEQ // MERGE_TILE)
        x_lat = _ffn(x_lat, mod_lat, *ffn2, ln_g, ln_b, layer=l, mod_idx=2, ln_idx=2, tiles_per_group=lat_tiles)

    kbt, vb, kct, vct = kv_ctx
    new_diff_k = kbt.reshape(BATCH, DEPTH, DIFF_HEADS, 2, HEAD_DIM, SEQ).transpose(0, 1, 5, 2, 3, 4)
    new_diff_v = vb.reshape(BATCH, DEPTH, SEQ, DIFF_HEADS, 2 * HEAD_DIM)
    new_na_k = kct.reshape(BATCH, DEPTH, NA_HEADS, HEAD_DIM, SEQ).transpose(0, 1, 4, 2, 3)
    new_na_v = vct.reshape(BATCH, DEPTH, NA_HEADS, HEAD_DIM, SEQ).transpose(0, 1, 4, 2, 3)
    return (x_ctx.reshape(BATCH, SEQ, D_MODEL), x_lat.reshape(DEC_BATCH, DEC_SEQ, D_MODEL),
            new_diff_k, new_diff_v, new_na_k, new_na_v)
```
